```python
import jax
import jax.numpy as jnp
from jax import lax
import numpy as np

D_MODEL = 2048
BATCH = 1
SEQ = 8192
DEPTH = 1
DEC_BATCH = 8
DEC_SEQ = 4096
PAST_LEN = 128

GRID_W = 64
NA_HEADS = 8
NA_HEAD_DIM = 128
NA_WIDTH = NA_HEADS * NA_HEAD_DIM
NA_KH_MAX = 8
NA_KW = 16
NA_QB = 16
NA_KB = 32
NA_NCB = GRID_W // NA_QB
ML_HEADS = 4
ML_HEAD_DIM = 256
ML_WIDTH = ML_HEADS * ML_HEAD_DIM
ML_CHUNK = 128
ML_CONV = 5
D_FF = 5632
N_GATE = 4 * ML_HEADS
SPLIT_SIZES = (NA_WIDTH, NA_WIDTH, NA_WIDTH, 2 * ML_WIDTH, ML_WIDTH, ML_WIDTH, N_GATE, D_MODEL, D_MODEL)
D_IN = sum(SPLIT_SIZES)
ALPHA = (2 * DEPTH) ** 0.25
BETA = (8 * DEPTH) ** -0.25
LN_EPS = 1e-5

kernel_name = "hybrid_natten_mlstm_macaron_deepnorm_encoder"


def _split_points():
    return np.cumsum(np.array(SPLIT_SIZES))[:-1].tolist()


def _layernorm(x, g, b):
    xf = x.astype(jnp.float32)
    mu = xf.mean(-1, keepdims=True)
    var = jnp.square(xf - mu).mean(-1, keepdims=True)
    return ((xf - mu) * lax.rsqrt(var + LN_EPS) * g + b).astype(x.dtype)


def _swiglu(x, w_gu, w_down):
    g, u = jnp.split(x @ w_gu, 2, axis=-1)
    return (jax.nn.silu(g) * u) @ w_down


def _na_col_tables():
    qc = np.arange(GRID_W).reshape(NA_NCB, NA_QB)
    kstart = np.clip(np.arange(NA_NCB) * NA_QB - NA_KW // 2, 0, GRID_W - NA_KB)
    kc = kstart[:, None] + np.arange(NA_KB)[None, :]
    ws = np.clip(qc - NA_KW // 2, 0, GRID_W - NA_KW)
    kcb = kc[:, None, :]
    valid = (kcb >= ws[:, :, None]) & (kcb < ws[:, :, None] + NA_KW)
    col_idx = np.clip(kcb - qc[:, :, None] + NA_KW - 1, 0, 2 * NA_KW - 2)
    return kc, col_idx, valid


def _neighbourhood_attention(q, k, v, rpb):
    B, N, _ = q.shape
    rows = N // GRID_W
    kh = min(NA_KH_MAX, rows)

    def to_grid(t):
        return t.reshape(B, rows, GRID_W, NA_HEADS, NA_HEAD_DIM).transpose(0, 3, 1, 2, 4)

    qg = to_grid(q) * (NA_HEAD_DIM ** -0.5)
    kg = to_grid(k)
    vg = to_grid(v)
    kc, col_idx, valid = _na_col_tables()
    kc_flat = jnp.asarray(kc.reshape(-1))
    bias_col = rpb[:, :, col_idx].astype(jnp.float32)
    valid_m = jnp.asarray(valid)[None, None, :, :, None, :]
    q_rows = qg.reshape(B, NA_HEADS, rows, NA_NCB, NA_QB, NA_HEAD_DIM).transpose(2, 0, 1, 3, 4, 5)

    def one_row(args):
        r, qr = args
        rs = jnp.clip(r - kh // 2, 0, rows - kh)
        kb = jnp.take(lax.dynamic_slice_in_dim(kg, rs, kh, axis=2), kc_flat, axis=3)
        vb = jnp.take(lax.dynamic_slice_in_dim(vg, rs, kh, axis=2), kc_flat, axis=3)
        kb = kb.reshape(B, NA_HEADS, kh, NA_NCB, NA_KB, NA_HEAD_DIM)
        vb = vb.reshape(B, NA_HEADS, kh, NA_NCB, NA_KB, NA_HEAD_DIM)
        s = jnp.einsum('bhcqd,bhackd->bhcqak', qr, kb).astype(jnp.float32)
        row_idx = rs + jnp.arange(kh) - r + (NA_KH_MAX - 1)
        bias = jnp.take(bias_col, row_idx, axis=1).transpose(0, 2, 3, 1, 4)
        s = jnp.where(valid_m, s + bias[None], -jnp.inf)
        p = jax.nn.softmax(s.reshape(B, NA_HEADS, NA_NCB, NA_QB, kh * NA_KB), axis=-1)
        p = p.reshape(B, NA_HEADS, NA_NCB, NA_QB, kh, NA_KB).astype(vb.dtype)
        return jnp.einsum('bhcqak,bhackd->bhcqd', p, vb)

    out = lax.map(one_row, (jnp.arange(rows), q_rows))
    return out.transpose(1, 0, 3, 4, 2, 5).reshape(B, N, NA_WIDTH)


def _to_chunks(t, nc):
    G, H = t.shape[:2]
    return jnp.moveaxis(t.reshape((G, H, nc, ML_CHUNK) + t.shape[3:]), 2, 0)


def _mlstm_chunkwise(q, k, v, log_i, log_f):
    G, H, N, d = q.shape
    nc = N // ML_CHUNK
    causal = jnp.tril(jnp.ones((ML_CHUNK, ML_CHUNK), dtype=bool))

    def step(carry, inp):
        C, n, m = carry
        qb, kb, vb, ib, fb = inp
        b = jnp.cumsum(fb, axis=-1)
        dmat = jnp.where(causal, b[..., :, None] - b[..., None, :] + ib[..., None, :], -jnp.inf)
        inter = b + m[..., None]
        mt = jnp.maximum(inter, dmat.max(-1))
        sc = jnp.einsum('ghtd,ghsd->ghts', qb, kb) * jnp.exp(dmat - mt[..., None])
        a_inter = jnp.exp(inter - mt)
        num = jnp.einsum('ghts,ghsd->ghtd', sc, vb) + a_inter[..., None] * jnp.einsum('ghtd,ghde->ghte', qb, C)
        den = sc.sum(-1) + a_inter * jnp.einsum('ghtd,ghd->ght', qb, n)
        h = num / jnp.maximum(jnp.abs(den), jnp.exp(-mt))[..., None]
        b_last = b[..., -1]
        g = b_last[..., None] - b + ib
        m_new = jnp.maximum(b_last + m, g.max(-1))
        wk = kb * jnp.exp(g - m_new[..., None])[..., None]
        decay = jnp.exp(b_last + m - m_new)
        C_new = decay[..., None, None] * C + jnp.einsum('ghsd,ghse->ghde', wk, vb)
        n_new = decay[..., None] * n + wk.sum(axis=2)
        return (C_new, n_new, m_new), h

    init = (jnp.zeros((G, H, d, d), jnp.float32), jnp.zeros((G, H, d), jnp.float32), jnp.zeros((G, H), jnp.float32))
    xs = (_to_chunks(q, nc), _to_chunks(k, nc), _to_chunks(v, nc), _to_chunks(log_i, nc), _to_chunks(log_f, nc))
    _, hs = lax.scan(step, init, xs)
    return jnp.moveaxis(hs, 0, 2).reshape(G, H, N, d)


def _mlstm_branch(qk_raw, v, o, gates, conv_w, conv_b, gate_b, norm_g):
    B, N, _ = v.shape
    pad = ML_CONV // 2
    qk = lax.conv_general_dilated(qk_raw, conv_w[:, None, :], window_strides=(1,), padding=[(pad, pad)],
                                  dimension_numbers=('NWC', 'WIO', 'NWC'),
                                  feature_group_count=2 * ML_WIDTH) + conv_b
    qk = jax.nn.silu(qk).astype(jnp.float32)
    q, k = jnp.split(qk, 2, axis=-1)

    def heads(t):
        return t.reshape(B, N, ML_HEADS, ML_HEAD_DIM).transpose(0, 2, 1, 3)

    q = heads(q)
    k = heads(k) * (ML_HEAD_DIM ** -0.5)
    vh = heads(v.astype(jnp.float32))
    gt = (gates.astype(jnp.float32) + gate_b.astype(jnp.float32)).transpose(0, 2, 1)
    i_f, i_b, f_f, f_b = jnp.split(gt, 4, axis=1)
    flip = lambda t: jnp.flip(t, axis=2)
    qq = jnp.concatenate([q, flip(q)], axis=0)
    kk = jnp.concatenate([k, flip(k)], axis=0)
    vv = jnp.concatenate([vh, flip(vh)], axis=0)
    li = jnp.concatenate([i_f, flip(i_b)], axis=0)
    lf = jax.nn.log_sigmoid(jnp.concatenate([f_f, flip(f_b)], axis=0))
    h = _mlstm_chunkwise(qq, kk, vv, li, lf)
    h = h[:B] + flip(h[B:])
    mu = h.mean(-1, keepdims=True)
    var = jnp.square(h - mu).mean(-1, keepdims=True)
    h = (h - mu) * lax.rsqrt(var + LN_EPS) * norm_g.astype(jnp.float32).reshape(ML_HEADS, ML_HEAD_DIM)[None, :, None, :]
    h = h.transpose(0, 2, 1, 3).reshape(B, N, ML_WIDTH)
    return (h * jax.nn.sigmoid(o.astype(jnp.float32))).astype(v.dtype)


def _mixer(x, w_in, rpb, conv_w, conv_b, gate_b, ml_norm_g, w_pa, w_pm, w_out):
    z = x @ w_in
    q_a, k_a, v_a, qk_m, v_m, o_m, gates, g_a, g_m = jnp.split(z, _split_points(), axis=-1)
    y_a = _neighbourhood_attention(q_a, k_a, v_a, rpb)
    y_m = _mlstm_branch(qk_m, v_m, o_m, gates, conv_w, conv_b, gate_b, ml_norm_g)
    merged = jax.nn.sigmoid(g_a) * (y_a @ w_pa) + jax.nn.sigmoid(g_m) * (y_m @ w_pm)
    return merged @ w_out


def _encoder(x, params):
    (ffa_w_gu, ffa_w_down, norm_a_g, norm_a_b, mix_w_in, na_rpb, ml_conv_w, ml_conv_b, ml_gate_b,
     ml_norm_g, mix_w_pa, mix_w_pm, mix_w_out, norm_m_g, norm_m_b, ffb_w_gu, ffb_w_down,
     norm_b_g, norm_b_b) = params
    for l in range(DEPTH):
        x = _layernorm(ALPHA * x + 0.5 * _swiglu(x, ffa_w_gu[l], ffa_w_down[l]), norm_a_g[l], norm_a_b[l])
        mix = _mixer(x, mix_w_in[l], na_rpb[l], ml_conv_w[l], ml_conv_b[l], ml_gate_b[l], ml_norm_g[l],
                     mix_w_pa[l], mix_w_pm[l], mix_w_out[l])
        x = _layernorm(ALPHA * x + mix, norm_m_g[l], norm_m_b[l])
        x = _layernorm(ALPHA * x + 0.5 * _swiglu(x, ffb_w_gu[l], ffb_w_down[l]), norm_b_g[l], norm_b_b[l])
    return x


def setup_inputs(seed: int = 0) -> dict:
    key = jax.random.key(seed)
    ks = jax.random.split(key, 24)
    f32 = jnp.float32

    def nrm(k, shape, scale):
        return jax.random.normal(k, shape, f32) * scale

    i_bias = nrm(ks[10], (DEPTH, 2 * ML_HEADS), 0.1)
    f_bias = jnp.tile(jnp.linspace(3.0, 6.0, ML_HEADS, dtype=f32), (DEPTH, 2)) + nrm(ks[11], (DEPTH, 2 * ML_HEADS), 0.01)
    return {
        "x_prompt": nrm(ks[0], (BATCH, SEQ, D_MODEL), 1.0),
        "x_sample": nrm(ks[1], (DEC_BATCH, DEC_SEQ, D_MODEL), 1.0),
        "ffa_w_gu": nrm(ks[2], (DEPTH, D_MODEL, 2 * D_FF), D_MODEL ** -0.5),
        "ffa_w_down": nrm(ks[3], (DEPTH, D_FF, D_MODEL), BETA * D_FF ** -0.5),
        "norm_a_g": 1.0 + nrm(ks[4], (DEPTH, D_MODEL), 0.02),
        "norm_a_b": nrm(ks[5], (DEPTH, D_MODEL), 0.02),
        "mix_w_in": nrm(ks[6], (DEPTH, D_MODEL, D_IN), D_MODEL ** -0.5),
        "na_rpb": nrm(ks[7], (DEPTH, NA_HEADS, 2 * NA_KH_MAX - 1, 2 * NA_KW - 1), 0.02),
        "ml_conv_w": nrm(ks[8], (DEPTH, ML_CONV, 2 * ML_WIDTH), ML_CONV ** -0.5),
        "ml_conv_b": nrm(ks[9], (DEPTH, 2 * ML_WIDTH), 0.02),
        "ml_gate_b": jnp.concatenate([i_bias, f_bias], axis=-1),
        "ml_norm_g": 1.0 + nrm(ks[12], (DEPTH, ML_WIDTH), 0.02),
        "mix_w_pa": nrm(ks[13], (DEPTH, NA_WIDTH, D_MODEL), NA_WIDTH ** -0.5),
        "mix_w_pm": nrm(ks[14], (DEPTH, ML_WIDTH, D_MODEL), ML_WIDTH ** -0.5),
        "mix_w_out": nrm(ks[15], (DEPTH, D_MODEL, D_MODEL), BETA * D_MODEL ** -0.5),
        "norm_m_g": 1.0 + nrm(ks[16], (DEPTH, D_MODEL), 0.02),
        "norm_m_b": nrm(ks[17], (DEPTH, D_MODEL), 0.02),
        "ffb_w_gu": nrm(ks[18], (DEPTH, D_MODEL, 2 * D_FF), D_MODEL ** -0.5),
        "ffb_w_down": nrm(ks[19], (DEPTH, D_FF, D_MODEL), BETA * D_FF ** -0.5),
        "norm_b_g": 1.0 + nrm(ks[20], (DEPTH, D_MODEL), 0.02),
        "norm_b_b": nrm(ks[21], (DEPTH, D_MODEL), 0.02),
    }


def reference(x_prompt, x_sample, ffa_w_gu, ffa_w_down, norm_a_g, norm_a_b, mix_w_in, na_rpb,
              ml_conv_w, ml_conv_b, ml_gate_b, ml_norm_g, mix_w_pa, mix_w_pm, mix_w_out,
              norm_m_g, norm_m_b, ffb_w_gu, ffb_w_down, norm_b_g, norm_b_b):
    params = (ffa_w_gu, ffa_w_down, norm_a_g, norm_a_b, mix_w_in, na_rpb, ml_conv_w, ml_conv_b,
              ml_gate_b, ml_norm_g, mix_w_pa, mix_w_pm, mix_w_out, norm_m_g, norm_m_b,
              ffb_w_gu, ffb_w_down, norm_b_g, norm_b_b)
    y_prompt = _encoder(x_prompt, params)
    y_sample = _encoder(x_sample, params)
    return (y_prompt, y_sample)
```

```python
import functools

import jax
import jax.numpy as jnp
import numpy as np
from jax import lax
from jax.experimental import pallas as pl
from jax.experimental.pallas import tpu as pltpu

D_MODEL = 2048
D_FF = 5632
DEPTH = 1
ALPHA = (2 * DEPTH) ** 0.25
LN_EPS = 1e-5

GRID_W = 64
NA_HEADS = 8
NA_HEAD_DIM = 128
NA_WIDTH = NA_HEADS * NA_HEAD_DIM
NA_KH = 8
NA_KW = 16
NA_QROWS = 8
NA_KROWS = 16
NA_RPB_ROWS = 2 * NA_KH - 1
NA_RPB_COLS = 2 * NA_KW - 1

ML_HEADS = 4
ML_HEAD_DIM = 256
ML_WIDTH = ML_HEADS * ML_HEAD_DIM
ML_CHUNK = 128
ML_CONV = 5
N_GATE = 4 * ML_HEADS

UNIT = 1024
U_G_NA, U_G_ML, U_Q_NA, U_K_NA, U_V_NA, U_Q_ML, U_K_ML, U_V_ML, U_O_ML = 0, 2, 4, 5, 6, 7, 8, 9, 10
Z_WIDTH = 11 * UNIT
GATE_PAD = 128

NEG_BIG = -1e30
VMEM_LIMIT = 56 * 1024 * 1024

F32 = jnp.float32
BF16 = jnp.bfloat16


def _cparams(n_grid_dims):
    return pltpu.CompilerParams(dimension_semantics=("arbitrary",) * n_grid_dims,
                                vmem_limit_bytes=VMEM_LIMIT)


def _sigmoid(x):
    return 1.0 / (1.0 + jnp.exp(-x))


def _layernorm_rows(y, g, b):
    mu = jnp.mean(y, axis=-1, keepdims=True)
    yc = y - mu
    var = jnp.mean(yc * yc, axis=-1, keepdims=True)
    return yc * lax.rsqrt(var + LN_EPS) * g + b


def _ffn_ln_kernel(x_ref, wg_ref, wu_ref, wd_ref, g_ref, b_ref, o_ref, xb_ref, acc_ref):
    j = pl.program_id(1)

    @pl.when(j == 0)
    def _():
        xb_ref[...] = x_ref[...].astype(BF16)
        acc_ref[...] = jnp.zeros_like(acc_ref)

    xb = xb_ref[...]
    gate = jnp.dot(xb, wg_ref[...], preferred_element_type=F32)
    up = jnp.dot(xb, wu_ref[...], preferred_element_type=F32)
    act = gate * _sigmoid(gate) * up
    acc_ref[...] += jnp.dot(act.astype(BF16), wd_ref[...], preferred_element_type=F32)

    @pl.when(j == pl.num_programs(1) - 1)
    def _():
        y = ALPHA * x_ref[...] + 0.5 * acc_ref[...]
        o_ref[...] = _layernorm_rows(y, g_ref[...], b_ref[...])


def _ffn_ln(x, w_gu, w_down, g, b, *, tm=512, tf=512):
    t, d = x.shape
    d_ff = w_down.shape[0]
    n_ff = d_ff // tf
    assert t % tm == 0 and d_ff % tf == 0
    return pl.pallas_call(
        _ffn_ln_kernel,
        grid=(t // tm, n_ff),
        in_specs=[
            pl.BlockSpec((tm, d), lambda i, j: (i, 0)),
            pl.BlockSpec((d, tf), lambda i, j: (0, j)),
            pl.BlockSpec((d, tf), lambda i, j: (0, j + n_ff)),
            pl.BlockSpec((tf, d), lambda i, j: (j, 0)),
            pl.BlockSpec((1, d), lambda i, j: (0, 0)),
            pl.BlockSpec((1, d), lambda i, j: (0, 0)),
        ],
        out_specs=pl.BlockSpec((tm, d), lambda i, j: (i, 0)),
        out_shape=jax.ShapeDtypeStruct((t, d), F32),
        scratch_shapes=[pltpu.VMEM((tm, d), BF16), pltpu.VMEM((tm, d), F32)],
        compiler_params=_cparams(2),
        name="ffn_ln",
    )(x, w_gu, w_gu, w_down, g, b)


def _in_proj_kernel(x_ref, w_ref, wgate_ref, z_ref, gates_ref, xb_ref):
    j = pl.program_id(1)

    @pl.when(j == 0)
    def _():
        xb_ref[...] = x_ref[...].astype(BF16)
        gates_ref[...] = jnp.dot(xb_ref[...], wgate_ref[...], preferred_element_type=F32)

    z_ref[...] = jnp.dot(xb_ref[...], w_ref[...], preferred_element_type=F32)


def _in_proj(x, w, w_gate, *, tm=512, tn=1024):
    t, d = x.shape
    n = w.shape[1]
    assert t % tm == 0 and n % tn == 0
    return pl.pallas_call(
        _in_proj_kernel,
        grid=(t // tm, n // tn),
        in_specs=[
            pl.BlockSpec((tm, d), lambda i, j: (i, 0)),
            pl.BlockSpec((d, tn), lambda i, j: (0, j)),
            pl.BlockSpec((d, GATE_PAD), lambda i, j: (0, 0)),
        ],
        out_specs=[
            pl.BlockSpec((tm, tn), lambda i, j: (i, j)),
            pl.BlockSpec((tm, GATE_PAD), lambda i, j: (i, 0)),
        ],
        out_shape=[jax.ShapeDtypeStruct((t, n), F32), jax.ShapeDtypeStruct((t, GATE_PAD), F32)],
        scratch_shapes=[pltpu.VMEM((tm, d), BF16)],
        compiler_params=_cparams(2),
        name="in_proj",
    )(x, w, w_gate)


def _na_bias_tables(rpb):
    a = np.arange(NA_QROWS)[:, None, None, None]
    c = np.arange(GRID_W)[None, :, None, None]
    e = np.arange(NA_KROWS)[None, None, :, None]
    kc = np.arange(GRID_W)[None, None, None, :]
    ws = np.clip(c - NA_KW // 2, 0, GRID_W - NA_KW)
    col_ok = (kc >= ws) & (kc < ws + NA_KW)
    col_idx = np.clip(kc - c + NA_KW - 1, 0, NA_RPB_COLS - 1)
    tables = []
    for ty in range(3):
        if ty == 0:
            start, q_off = np.maximum(a - NA_KH // 2, 0), 0
        elif ty == 1:
            start, q_off = a, NA_KH // 2
        else:
            start, q_off = np.minimum(a + NA_KH // 2, NA_KROWS - NA_KH), NA_KROWS - NA_QROWS
        row_ok = (e >= start) & (e < start + NA_KH)
        row_idx = np.clip(e - (a + q_off) + NA_KH - 1, 0, NA_RPB_ROWS - 1)
        ok = np.broadcast_to(row_ok & col_ok, (NA_QROWS, GRID_W, NA_KROWS, GRID_W))
        ri = np.broadcast_to(row_idx, ok.shape).reshape(NA_QROWS * GRID_W, NA_KROWS * GRID_W)
        ci = np.broadcast_to(col_idx, ok.shape).reshape(NA_QROWS * GRID_W, NA_KROWS * GRID_W)
        ok = ok.reshape(NA_QROWS * GRID_W, NA_KROWS * GRID_W)
        bias = rpb[:, ri, ci].astype(F32)
        tables.append(jnp.where(jnp.asarray(ok)[None], bias, NEG_BIG))
    return jnp.stack(tables)


def _na_kernel(q_ref, k_ref, v_ref, tab_ref, o_ref, *, rows):
    i = pl.program_id(2)
    first_key_row = jnp.clip(NA_QROWS * i - NA_KH // 2, 0, rows - NA_KROWS)
    start = pl.multiple_of(first_key_row * GRID_W, GRID_W * (NA_KH // 2))
    nk = NA_KROWS * GRID_W
    k = k_ref[pl.ds(start, nk), :].astype(BF16)
    v = v_ref[pl.ds(start, nk), :].astype(BF16)
    q = (q_ref[...] * (NA_HEAD_DIM ** -0.5)).astype(BF16)
    s = lax.dot_general(q, k, (((1,), (1,)), ((), ())), preferred_element_type=F32)
    s = s + tab_ref[...]
    m = jnp.max(s, axis=-1, keepdims=True)
    p = jnp.exp(s - m)
    l = jnp.sum(p, axis=-1, keepdims=True)
    o = jnp.dot(p.astype(BF16), v, preferred_element_type=F32)
    o_ref[...] = (o / l).astype(o_ref.dtype)


def _neighbourhood_attention(z, tables, *, batch, seq):
    rows = seq // GRID_W
    assert seq % GRID_W == 0 and rows % NA_QROWS == 0 and rows >= 2 * NA_KROWS
    nblk = rows // NA_QROWS
    nq = NA_QROWS * GRID_W
    hd = NA_HEAD_DIM
    per_unit = UNIT // hd

    def tab_map(b, h, i):
        ty = jnp.where(i == 0, 0, jnp.where(i == nblk - 1, 2, 1))
        return (ty, h, 0, 0)

    return pl.pallas_call(
        functools.partial(_na_kernel, rows=rows),
        grid=(batch, NA_HEADS, nblk),
        in_specs=[
            pl.BlockSpec((nq, hd), lambda b, h, i: (b * nblk + i, U_Q_NA * per_unit + h)),
            pl.BlockSpec((seq, hd), lambda b, h, i: (b, U_K_NA * per_unit + h)),
            pl.BlockSpec((seq, hd), lambda b, h, i: (b, U_V_NA * per_unit + h)),
            pl.BlockSpec((None, None, nq, NA_KROWS * GRID_W), tab_map),
        ],
        out_specs=pl.BlockSpec((nq, hd), lambda b, h, i: (b * nblk + i, h)),
        out_shape=jax.ShapeDtypeStruct((batch * seq, NA_WIDTH), BF16),
        compiler_params=_cparams(3),
        name="na",
    )(z, z, z, tables)


HALO = 8


def _conv_silu_kernel(prev_ref, cur_ref, next_ref, w_ref, b_ref, o_ref, ext_ref, *, tc):
    t = pl.program_id(1)
    part = pl.program_id(2)
    prev_keep = jnp.where(t == 0, 0.0, 1.0)
    next_keep = jnp.where(t == pl.num_programs(1) - 1, 0.0, 1.0)
    ext_ref[0:HALO, :] = prev_ref[...] * prev_keep
    ext_ref[HALO:HALO + tc, :] = cur_ref[...]
    ext_ref[HALO + tc:, :] = next_ref[...] * next_keep
    pad = ML_CONV // 2
    y = jnp.zeros((tc, cur_ref.shape[1]), F32) + b_ref[...]
    for j in range(ML_CONV):
        y = y + ext_ref[pl.ds(HALO - pad + j, tc), :] * w_ref[j:j + 1, :]
    y = y * _sigmoid(y)
    scale = jnp.where(part == 1, ML_HEAD_DIM ** -0.5, 1.0)
    o_ref[...] = (y * scale).astype(o_ref.dtype)


def _conv_silu(z, conv_w, conv_b, *, batch, seq, tc=512):
    assert seq % tc == 0 and tc % HALO == 0
    nt = seq // tc
    per = tc // HALO
    last_halo = batch * seq // HALO - 1

    return pl.pallas_call(
        functools.partial(_conv_silu_kernel, tc=tc),
        grid=(batch, nt, 2),
        in_specs=[
            pl.BlockSpec((HALO, UNIT), lambda b, t, p: (jnp.maximum((b * nt + t) * per - 1, 0), U_Q_ML + p)),
            pl.BlockSpec((tc, UNIT), lambda b, t, p: (b * nt + t, U_Q_ML + p)),
            pl.BlockSpec((HALO, UNIT), lambda b, t, p: (jnp.minimum((b * nt + t + 1) * per, last_halo), U_Q_ML + p)),
            pl.BlockSpec((ML_CONV, UNIT), lambda b, t, p: (0, p)),
            pl.BlockSpec((1, UNIT), lambda b, t, p: (0, p)),
        ],
        out_specs=pl.BlockSpec((tc, UNIT), lambda b, t, p: (b * nt + t, p)),
        out_shape=jax.ShapeDtypeStruct((batch * seq, 2 * ML_WIDTH), BF16),
        scratch_shapes=[pltpu.VMEM((tc + 2 * HALO, UNIT), F32)],
        compiler_params=_cparams(3),
        name="conv_silu",
    )(z, z, z, conv_w, conv_b)


def _log_sigmoid(x):
    return jnp.minimum(x, 0.0) - jnp.log(1.0 + jnp.exp(-jnp.abs(x)))


def _mlstm_kernel(q_ref, k_ref, v_ref, gcol_ref, grow_ref, bcol_ref, brow_ref, h_ref,
                  c_ref, n_ref, m_ref):
    d = pl.program_id(1)
    c = pl.program_id(2)
    L = ML_CHUNK
    dh = ML_HEAD_DIM

    @pl.when(c == 0)
    def _():
        c_ref[...] = jnp.zeros_like(c_ref)
        n_ref[...] = jnp.zeros_like(n_ref)
        m_ref[...] = jnp.zeros_like(m_ref)

    t_idx = lax.broadcasted_iota(jnp.int32, (L, L), 0)
    s_idx = lax.broadcasted_iota(jnp.int32, (L, L), 1)
    sign = 1 - 2 * d
    seen = (t_idx - s_idx) * sign >= 0
    seen_t = (s_idx - t_idx) * sign >= 0
    seen_f = seen.astype(F32)
    seen_t_f = seen_t.astype(F32)

    g_col = gcol_ref[...] + bcol_ref[...]
    g_row = grow_ref[...] + brow_ref[...]

    for h in range(ML_HEADS):
        sl = slice(h * dh, (h + 1) * dh)
        q = q_ref[:, sl]
        k = k_ref[:, sl]
        v = v_ref[:, sl].astype(BF16)
        i_col = g_col[:, h:h + 1]
        f_col = _log_sigmoid(g_col[:, ML_HEADS + h:ML_HEADS + h + 1])
        i_row = g_row[h:h + 1, :]
        f_row = _log_sigmoid(g_row[ML_HEADS + h:ML_HEADS + h + 1, :])
        m_prev = m_ref[h]
        c_prev = c_ref[h]
        n_prev = n_ref[h]

        b_col = jnp.sum(seen_f * f_row, axis=1, keepdims=True)
        b_row = jnp.sum(seen_t_f * f_col, axis=0, keepdims=True)
        b_last = jnp.sum(f_row, axis=1, keepdims=True)
        dmat = jnp.where(seen, b_col - b_row + i_row, NEG_BIG)
        inter = b_col + m_prev
        mt = jnp.maximum(inter, jnp.max(dmat, axis=1, keepdims=True))
        qk = lax.dot_general(q, k, (((1,), (1,)), ((), ())), preferred_element_type=F32)
        sc = qk * jnp.exp(dmat - mt)
        a_inter = jnp.exp(inter - mt)
        num = (jnp.dot(sc.astype(BF16), v, preferred_element_type=F32)
               + a_inter * jnp.dot(q, c_prev.astype(BF16), preferred_element_type=F32))
        den = (jnp.sum(sc, axis=1, keepdims=True)
               + a_inter * jnp.sum(q.astype(F32) * n_prev, axis=1, keepdims=True))
        h_ref[:, sl] = num / jnp.maximum(jnp.abs(den), jnp.exp(-mt))

        g_end = b_last - b_col + i_col
        m_new = jnp.maximum(b_last + m_prev, jnp.max(g_end, axis=0, keepdims=True))
        wk = k.astype(F32) * jnp.exp(g_end - m_new)
        decay = jnp.exp(b_last + m_prev - m_new)
        c_ref[h] = decay * c_prev + lax.dot_general(
            wk.astype(BF16), v, (((0,), (0,)), ((), ())), preferred_element_type=F32)
        n_ref[h] = decay * n_prev + jnp.sum(wk, axis=0, keepdims=True)
        m_ref[h] = m_new


def _mlstm(qk, z, gates_col, gates_row, bias_col, bias_row, *, batch, seq):
    assert seq % ML_CHUNK == 0
    nc = seq // ML_CHUNK
    L = ML_CHUNK

    def tok(b, d, c):
        return b * nc + c + d * (nc - 1 - 2 * c)

    return pl.pallas_call(
        _mlstm_kernel,
        grid=(batch, 2, nc),
        in_specs=[
            pl.BlockSpec((L, ML_WIDTH), lambda b, d, c: (tok(b, d, c), 0)),
            pl.BlockSpec((L, ML_WIDTH), lambda b, d, c: (tok(b, d, c), 1)),
            pl.BlockSpec((L, UNIT), lambda b, d, c: (tok(b, d, c), U_V_ML)),
            pl.BlockSpec((None, L, 2 * ML_HEADS), lambda b, d, c: (d, tok(b, d, c), 0)),
            pl.BlockSpec((None, 2 * ML_HEADS, L), lambda b, d, c: (d, 0, tok(b, d, c))),
            pl.BlockSpec((None, 1, 2 * ML_HEADS), lambda b, d, c: (d, 0, 0)),
            pl.BlockSpec((None, 2 * ML_HEADS, 1), lambda b, d, c: (d, 0, 0)),
        ],
        out_specs=pl.BlockSpec((None, L, ML_WIDTH), lambda b, d, c: (d, tok(b, d, c), 0)),
        out_shape=jax.ShapeDtypeStruct((2, batch * seq, ML_WIDTH), F32),
        scratch_shapes=[
            pltpu.VMEM((ML_HEADS, ML_HEAD_DIM, ML_HEAD_DIM), F32),
            pltpu.VMEM((ML_HEADS, 1, ML_HEAD_DIM), F32),
            pltpu.VMEM((ML_HEADS, 1, 1), F32),
        ],
        compiler_params=_cparams(3),
        name="mlstm",
    )(qk, qk, z, gates_col, gates_row, bias_col, bias_row)


def _merge_ln_kernel(x_ref, hf_ref, hb_ref, o_ref, ya_ref, ga_ref, gm_ref, ng_ref,
                     wpa_ref, wpm_ref, wout_ref, g_ref, b_ref, out_ref):
    h = hf_ref[...] + hb_ref[...]
    parts = []
    for hh in range(ML_HEADS):
        sl = slice(hh * ML_HEAD_DIM, (hh + 1) * ML_HEAD_DIM)
        hs = h[:, sl]
        mu = jnp.mean(hs, axis=-1, keepdims=True)
        hc = hs - mu
        var = jnp.mean(hc * hc, axis=-1, keepdims=True)
        parts.append(hc * lax.rsqrt(var + LN_EPS))
    y_m = jnp.concatenate(parts, axis=-1) * ng_ref[...] * _sigmoid(o_ref[...])
    pa = jnp.dot(ya_ref[...], wpa_ref[...], preferred_element_type=F32)
    pm = jnp.dot(y_m.astype(BF16), wpm_ref[...], preferred_element_type=F32)
    merged = _sigmoid(ga_ref[...]) * pa + _sigmoid(gm_ref[...]) * pm
    mix = jnp.dot(merged.astype(BF16), wout_ref[...], preferred_element_type=F32)
    out_ref[...] = _layernorm_rows(ALPHA * x_ref[...] + mix, g_ref[...], b_ref[...])


def _merge_ln(x, hdir, z, y_a, norm_g, w_pa, w_pm, w_out, g, b, *, tm=256):
    t, d = x.shape
    assert t % tm == 0
    wide = 2 * UNIT
    const = lambda i: (0, 0)
    return pl.pallas_call(
        _merge_ln_kernel,
        grid=(t // tm,),
        in_specs=[
            pl.BlockSpec((tm, d), lambda i: (i, 0)),
            pl.BlockSpec((None, tm, ML_WIDTH), lambda i: (0, i, 0)),
            pl.BlockSpec((None, tm, ML_WIDTH), lambda i: (1, i, 0)),
            pl.BlockSpec((tm, UNIT), lambda i: (i, U_O_ML)),
            pl.BlockSpec((tm, NA_WIDTH), lambda i: (i, 0)),
            pl.BlockSpec((tm, wide), lambda i: (i, U_G_NA // 2)),
            pl.BlockSpec((tm, wide), lambda i: (i, U_G_ML // 2)),
            pl.BlockSpec((1, ML_WIDTH), const),
            pl.BlockSpec((NA_WIDTH, d), const, pipeline_mode=pl.Buffered(1)),
            pl.BlockSpec((ML_WIDTH, d), const, pipeline_mode=pl.Buffered(1)),
            pl.BlockSpec((d, d), const, pipeline_mode=pl.Buffered(1)),
            pl.BlockSpec((1, d), const),
            pl.BlockSpec((1, d), const),
        ],
        out_specs=pl.BlockSpec((tm, d), lambda i: (i, 0)),
        out_shape=jax.ShapeDtypeStruct((t, d), F32),
        compiler_params=_cparams(1),
        name="merge_ln",
    )(x, hdir, hdir, z, y_a, z, z, norm_g, w_pa, w_pm, w_out, g, b)


def _prepare_params(ffa_w_gu, ffa_w_down, norm_a_g, norm_a_b, mix_w_in, na_rpb, ml_conv_w, ml_conv_b,
                    ml_gate_b, ml_norm_g, mix_w_pa, mix_w_pm, mix_w_out, norm_m_g, norm_m_b,
                    ffb_w_gu, ffb_w_down, norm_b_g, norm_b_b, l):
    sizes = (NA_WIDTH, NA_WIDTH, NA_WIDTH, 2 * ML_WIDTH, ML_WIDTH, ML_WIDTH, N_GATE, D_MODEL, D_MODEL)
    points = np.cumsum(sizes)[:-1].tolist()
    w_qa, w_ka, w_va, w_qkm, w_vm, w_om, w_gates, w_ga, w_gm = jnp.split(mix_w_in[l], points, axis=1)
    w_in = jnp.concatenate([w_ga, w_gm, w_qa, w_ka, w_va, w_qkm, w_vm, w_om], axis=1).astype(BF16)
    w_gate = jnp.pad(w_gates, ((0, 0), (0, GATE_PAD - N_GATE))).astype(BF16)
    gb = ml_gate_b[l].astype(F32).reshape(2, 2, ML_HEADS).transpose(1, 0, 2).reshape(2, 2 * ML_HEADS)
    row = lambda v: v.reshape(1, -1).astype(F32)
    return dict(
        ffa_w_gu=ffa_w_gu[l].astype(BF16), ffa_w_down=ffa_w_down[l].astype(BF16),
        norm_a_g=row(norm_a_g[l]), norm_a_b=row(norm_a_b[l]),
        w_in=w_in, w_gate=w_gate,
        na_tables=_na_bias_tables(na_rpb[l]),
        conv_w=ml_conv_w[l].astype(F32), conv_b=row(ml_conv_b[l]),
        gate_b_col=gb.reshape(2, 1, 2 * ML_HEADS), gate_b_row=gb.reshape(2, 2 * ML_HEADS, 1),
        ml_norm_g=row(ml_norm_g[l]),
        w_pa=mix_w_pa[l].astype(BF16), w_pm=mix_w_pm[l].astype(BF16), w_out=mix_w_out[l].astype(BF16),
        norm_m_g=row(norm_m_g[l]), norm_m_b=row(norm_m_b[l]),
        ffb_w_gu=ffb_w_gu[l].astype(BF16), ffb_w_down=ffb_w_down[l].astype(BF16),
        norm_b_g=row(norm_b_g[l]), norm_b_b=row(norm_b_b[l]),
    )


def _encoder_layer(x, p):
    batch, seq, d = x.shape
    t = batch * seq
    x0 = x.reshape(t, d)
    x1 = _ffn_ln(x0, p["ffa_w_gu"], p["ffa_w_down"], p["norm_a_g"], p["norm_a_b"])
    z, gates = _in_proj(x1, p["w_in"], p["w_gate"])
    y_a = _neighbourhood_attention(z, p["na_tables"], batch=batch, seq=seq)
    qk = _conv_silu(z, p["conv_w"], p["conv_b"], batch=batch, seq=seq)
    gd = gates[:, :N_GATE].reshape(t, 2, 2, ML_HEADS)
    gates_col = gd.transpose(2, 0, 1, 3).reshape(2, t, 2 * ML_HEADS)
    gates_row = gd.transpose(2, 1, 3, 0).reshape(2, 2 * ML_HEADS, t)
    hdir = _mlstm(qk, z, gates_col, gates_row, p["gate_b_col"], p["gate_b_row"], batch=batch, seq=seq)
    x2 = _merge_ln(x1, hdir, z, y_a, p["ml_norm_g"], p["w_pa"], p["w_pm"], p["w_out"],
                   p["norm_m_g"], p["norm_m_b"])
    y = _ffn_ln(x2, p["ffb_w_gu"], p["ffb_w_down"], p["norm_b_g"], p["norm_b_b"])
    return y.reshape(batch, seq, d)


def kernel(x_prompt, x_sample, ffa_w_gu, ffa_w_down, norm_a_g, norm_a_b, mix_w_in, na_rpb, ml_conv_w, ml_conv_b, ml_gate_b, ml_norm_g, mix_w_pa, mix_w_pm, mix_w_out, norm_m_g, norm_m_b, ffb_w_gu, ffb_w_down, norm_b_g, norm_b_b):
    assert ffa_w_gu.shape[0] == DEPTH
    y_prompt, y_sample = x_prompt, x_sample
    for l in range(DEPTH):
        p = _prepare_params(ffa_w_gu, ffa_w_down, norm_a_g, norm_a_b, mix_w_in, na_rpb, ml_conv_w,
                            ml_conv_b, ml_gate_b, ml_norm_g, mix_w_pa, mix_w_pm, mix_w_out, norm_m_g,
                            norm_m_b, ffb_w_gu, ffb_w_down, norm_b_g, norm_b_b, l)
        y_prompt = _encoder_layer(y_prompt, p)
        y_sample = _encoder_layer(y_sample, p)
    return (y_prompt, y_sample)
```

```python
import functools

import jax
import jax.numpy as jnp
import numpy as np
from jax import lax
from jax.experimental import pallas as pl
from jax.experimental.pallas import tpu as pltpu

D_MODEL = 2048
D_FF = 5632
DEPTH = 1
ALPHA = (2 * DEPTH) ** 0.25
LN_EPS = 1e-5

GRID_W = 64
NA_HEADS = 8
NA_HEAD_DIM = 128
NA_WIDTH = NA_HEADS * NA_HEAD_DIM
NA_KH = 8
NA_KW = 16
NA_QROWS = 4
NA_KROWS = 12
NA_BLOCKS_PER_STEP = 2
NA_RPB_ROWS = 2 * NA_KH - 1
NA_RPB_COLS = 2 * NA_KW - 1

ML_HEADS = 4
ML_HEAD_DIM = 256
ML_WIDTH = ML_HEADS * ML_HEAD_DIM
ML_CHUNK = 128
ML_CONV = 5
N_GATE = 4 * ML_HEADS

UNIT = 1024
U32_G_NA, U32_G_ML, U32_Q_ML, U32_K_ML, U32_O_ML = 0, 2, 4, 5, 6
N_UNITS32 = 7
U16_Q_NA, U16_K_NA, U16_V_NA, U16_V_ML = 0, 1, 2, 3
N_UNITS16 = 4
GATE_PAD = 128

NEG_BIG = -1e30
VMEM_LIMIT = 56 * 1024 * 1024

F32 = jnp.float32
BF16 = jnp.bfloat16


def _cparams(n_grid_dims):
    return pltpu.CompilerParams(dimension_semantics=("arbitrary",) * n_grid_dims,
                                vmem_limit_bytes=VMEM_LIMIT)


def _sigmoid(x):
    return 1.0 / (1.0 + jnp.exp(-x))


def _layernorm_rows(y, g, b):
    mu = jnp.mean(y, axis=-1, keepdims=True)
    yc = y - mu
    var = jnp.mean(yc * yc, axis=-1, keepdims=True)
    return yc * lax.rsqrt(var + LN_EPS) * g + b


def _ffn_ln_kernel(x_ref, wgu_ref, wd_ref, g_ref, b_ref, o_ref, xb_ref, acc_ref, *, tf):
    j = pl.program_id(1)

    @pl.when(j == 0)
    def _():
        xb_ref[...] = x_ref[...].astype(BF16)
        acc_ref[...] = jnp.zeros_like(acc_ref)

    gu = jnp.dot(xb_ref[...], wgu_ref[...], preferred_element_type=F32)
    gate = gu[:, :tf]
    up = gu[:, tf:]
    act = gate * _sigmoid(gate) * up
    acc_ref[...] += jnp.dot(act.astype(BF16), wd_ref[...], preferred_element_type=F32)

    @pl.when(j == pl.num_programs(1) - 1)
    def _():
        y = ALPHA * x_ref[...] + 0.5 * acc_ref[...]
        o_ref[...] = _layernorm_rows(y, g_ref[...], b_ref[...])


def _block_gu(w_gu, tf):
    d, two_ff = w_gu.shape
    d_ff = two_ff // 2
    n_ff = d_ff // tf
    wg = w_gu[:, :d_ff].reshape(d, n_ff, tf)
    wu = w_gu[:, d_ff:].reshape(d, n_ff, tf)
    return jnp.concatenate([wg, wu], axis=2).transpose(1, 0, 2)


FFN_TM = 512
FFN_TF = 512


def _ffn_ln(x, w_gu_blocked, w_down, g, b):
    t, d = x.shape
    n_ff, _, two_tf = w_gu_blocked.shape
    tf = two_tf // 2
    tm = FFN_TM
    assert t % tm == 0 and w_down.shape[0] == n_ff * tf
    return pl.pallas_call(
        functools.partial(_ffn_ln_kernel, tf=tf),
        grid=(t // tm, n_ff),
        in_specs=[
            pl.BlockSpec((tm, d), lambda i, j: (i, 0)),
            pl.BlockSpec((None, d, two_tf), lambda i, j: (j, 0, 0)),
            pl.BlockSpec((tf, d), lambda i, j: (j, 0)),
            pl.BlockSpec((1, d), lambda i, j: (0, 0)),
            pl.BlockSpec((1, d), lambda i, j: (0, 0)),
        ],
        out_specs=pl.BlockSpec((tm, d), lambda i, j: (i, 0)),
        out_shape=jax.ShapeDtypeStruct((t, d), F32),
        scratch_shapes=[pltpu.VMEM((tm, d), BF16), pltpu.VMEM((tm, d), F32)],
        compiler_params=_cparams(2),
        name="ffn_ln",
    )(x, w_gu_blocked, w_down, g, b)


PROJ_TM = 512


def _in_proj32_kernel(x_ref, w_ref, wgate_ref, z_ref, gates_ref, xb_ref):
    @pl.when(pl.program_id(1) == 0)
    def _():
        xb_ref[...] = x_ref[...].astype(BF16)
        gates_ref[...] = jnp.dot(xb_ref[...], wgate_ref[...], preferred_element_type=F32)

    z_ref[...] = jnp.dot(xb_ref[...], w_ref[...], preferred_element_type=F32)


def _in_proj32(x, w_blocked, w_gate):
    t, d = x.shape
    n_units = w_blocked.shape[0]
    tm = PROJ_TM
    assert t % tm == 0
    return pl.pallas_call(
        _in_proj32_kernel,
        grid=(t // tm, n_units),
        in_specs=[
            pl.BlockSpec((tm, d), lambda i, j: (i, 0)),
            pl.BlockSpec((None, d, UNIT), lambda i, j: (j, 0, 0)),
            pl.BlockSpec((d, GATE_PAD), lambda i, j: (0, 0)),
        ],
        out_specs=[
            pl.BlockSpec((tm, UNIT), lambda i, j: (i, j)),
            pl.BlockSpec((tm, GATE_PAD), lambda i, j: (i, 0)),
        ],
        out_shape=[jax.ShapeDtypeStruct((t, n_units * UNIT), F32),
                   jax.ShapeDtypeStruct((t, GATE_PAD), F32)],
        scratch_shapes=[pltpu.VMEM((tm, d), BF16)],
        compiler_params=_cparams(2),
        name="in_proj32",
    )(x, w_blocked, w_gate)


def _in_proj16_kernel(x_ref, w_ref, z_ref, xb_ref):
    j = pl.program_id(1)

    @pl.when(j == 0)
    def _():
        xb_ref[...] = x_ref[...].astype(BF16)

    scale = jnp.where(j == U16_Q_NA, NA_HEAD_DIM ** -0.5, 1.0)
    z = jnp.dot(xb_ref[...], w_ref[...], preferred_element_type=F32)
    z_ref[...] = (z * scale).astype(z_ref.dtype)


def _in_proj16(x, w_blocked):
    t, d = x.shape
    n_units = w_blocked.shape[0]
    tm = PROJ_TM
    assert t % tm == 0
    return pl.pallas_call(
        _in_proj16_kernel,
        grid=(t // tm, n_units),
        in_specs=[
            pl.BlockSpec((tm, d), lambda i, j: (i, 0)),
            pl.BlockSpec((None, d, UNIT), lambda i, j: (j, 0, 0)),
        ],
        out_specs=pl.BlockSpec((tm, UNIT), lambda i, j: (i, j)),
        out_shape=jax.ShapeDtypeStruct((t, n_units * UNIT), BF16),
        scratch_shapes=[pltpu.VMEM((tm, d), BF16)],
        compiler_params=_cparams(2),
        name="in_proj16",
    )(x, w_blocked)


def _na_bias_tables(rpb):
    c = np.arange(GRID_W)[:, None]
    kc = np.arange(GRID_W)[None, :]
    ws = np.clip(c - NA_KW // 2, 0, GRID_W - NA_KW)
    col_ok = (kc >= ws) & (kc < ws + NA_KW)
    col_idx = np.clip(kc - c + NA_KW - 1, 0, NA_RPB_COLS - 1)
    colb = jnp.where(jnp.asarray(col_ok)[None, None], rpb[:, :, col_idx].astype(F32), NEG_BIG)
    heads = rpb.shape[0]

    def masked(n_rows):
        return jnp.full((heads, n_rows, GRID_W, GRID_W), NEG_BIG, F32)

    tables = []
    for ty in range(3):
        per_query_row = []
        for a in range(NA_QROWS):
            if ty == 0:
                start, q_row = 0, a
            elif ty == 1:
                start, q_row = a, a + NA_KH // 2
            else:
                start, q_row = NA_KROWS - NA_KH, a + NA_KROWS - NA_QROWS
            first_rpb_row = start - q_row + NA_KH - 1
            assert 0 <= first_rpb_row and first_rpb_row + NA_KH <= NA_RPB_ROWS
            per_query_row.append(jnp.concatenate(
                [masked(start), colb[:, first_rpb_row:first_rpb_row + NA_KH],
                 masked(NA_KROWS - NA_KH - start)], axis=1))
        tab = jnp.stack(per_query_row, axis=1)
        tables.append(tab.transpose(0, 1, 3, 2, 4).reshape(heads, NA_QROWS * GRID_W, NA_KROWS * GRID_W))
    return jnp.stack(tables)


def _na_kernel(q_ref, k_ref, v_ref, tab0_ref, tab1_ref, o_ref, *, rows):
    i = pl.program_id(2)
    nq = NA_QROWS * GRID_W
    nk = NA_KROWS * GRID_W
    for blk, tab_ref in enumerate((tab0_ref, tab1_ref)):
        q_row0 = NA_QROWS * (NA_BLOCKS_PER_STEP * i + blk)
        first_key_row = jnp.clip(q_row0 - NA_KH // 2, 0, rows - NA_KROWS)
        start = pl.multiple_of(first_key_row * GRID_W, GRID_W * NA_QROWS)
        k = k_ref[pl.ds(start, nk), :]
        v = v_ref[pl.ds(start, nk), :]
        q = q_ref[blk * nq:(blk + 1) * nq, :]
        s = lax.dot_general(q, k, (((1,), (1,)), ((), ())), preferred_element_type=F32)
        s = s + tab_ref[...]
        m = jnp.max(s, axis=-1, keepdims=True)
        p = jnp.exp(s - m)
        l = jnp.sum(p, axis=-1, keepdims=True)
        o = jnp.dot(p.astype(BF16), v, preferred_element_type=F32)
        o_ref[blk * nq:(blk + 1) * nq, :] = (o / l).astype(o_ref.dtype)


def _neighbourhood_attention(z16, tables, *, batch, seq):
    rows = seq // GRID_W
    step_rows = NA_QROWS * NA_BLOCKS_PER_STEP
    assert NA_BLOCKS_PER_STEP == 2 and NA_KROWS == NA_KH + NA_QROWS
    assert seq % GRID_W == 0 and rows % step_rows == 0 and rows >= 2 * NA_KROWS
    nstep = rows // step_rows
    nq_step = step_rows * GRID_W
    hd = NA_HEAD_DIM
    per_unit = UNIT // hd
    tab_shape = (None, None, NA_QROWS * GRID_W, NA_KROWS * GRID_W)

    return pl.pallas_call(
        functools.partial(_na_kernel, rows=rows),
        grid=(batch, NA_HEADS, nstep),
        in_specs=[
            pl.BlockSpec((nq_step, hd), lambda b, h, i: (b * nstep + i, U16_Q_NA * per_unit + h)),
            pl.BlockSpec((seq, hd), lambda b, h, i: (b, U16_K_NA * per_unit + h)),
            pl.BlockSpec((seq, hd), lambda b, h, i: (b, U16_V_NA * per_unit + h)),
            pl.BlockSpec(tab_shape, lambda b, h, i: (jnp.where(i == 0, 0, 1), h, 0, 0)),
            pl.BlockSpec(tab_shape, lambda b, h, i: (jnp.where(i == nstep - 1, 2, 1), h, 0, 0)),
        ],
        out_specs=pl.BlockSpec((nq_step, hd), lambda b, h, i: (b * nstep + i, h)),
        out_shape=jax.ShapeDtypeStruct((batch * seq, NA_WIDTH), BF16),
        compiler_params=_cparams(3),
        name="na",
    )(z16, z16, z16, tables, tables)


HALO = 8


def _conv_silu_kernel(prev_ref, cur_ref, next_ref, w_ref, b_ref, o_ref, ext_ref, *, tc):
    t = pl.program_id(1)
    part = pl.program_id(2)
    prev_keep = jnp.where(t == 0, 0.0, 1.0)
    next_keep = jnp.where(t == pl.num_programs(1) - 1, 0.0, 1.0)
    ext_ref[0:HALO, :] = prev_ref[...] * prev_keep
    ext_ref[HALO:HALO + tc, :] = cur_ref[...]
    ext_ref[HALO + tc:, :] = next_ref[...] * next_keep
    pad = ML_CONV // 2
    y = jnp.zeros((tc, cur_ref.shape[1]), F32) + b_ref[...]
    for j in range(ML_CONV):
        y = y + ext_ref[pl.ds(HALO - pad + j, tc), :] * w_ref[j:j + 1, :]
    y = y * _sigmoid(y)
    scale = jnp.where(part == 1, ML_HEAD_DIM ** -0.5, 1.0)
    o_ref[...] = (y * scale).astype(o_ref.dtype)


def _conv_silu(z32, conv_w, conv_b, *, batch, seq, tc=512):
    assert seq % tc == 0 and tc % HALO == 0 and U32_K_ML == U32_Q_ML + 1
    nt = seq // tc
    per = tc // HALO
    last_halo = batch * seq // HALO - 1

    return pl.pallas_call(
        functools.partial(_conv_silu_kernel, tc=tc),
        grid=(batch, nt, 2),
        in_specs=[
            pl.BlockSpec((HALO, UNIT), lambda b, t, p: (jnp.maximum((b * nt + t) * per - 1, 0), U32_Q_ML + p)),
            pl.BlockSpec((tc, UNIT), lambda b, t, p: (b * nt + t, U32_Q_ML + p)),
            pl.BlockSpec((HALO, UNIT), lambda b, t, p: (jnp.minimum((b * nt + t + 1) * per, last_halo), U32_Q_ML + p)),
            pl.BlockSpec((ML_CONV, UNIT), lambda b, t, p: (0, p)),
            pl.BlockSpec((1, UNIT), lambda b, t, p: (0, p)),
        ],
        out_specs=pl.BlockSpec((tc, UNIT), lambda b, t, p: (b * nt + t, p)),
        out_shape=jax.ShapeDtypeStruct((batch * seq, 2 * ML_WIDTH), BF16),
        scratch_shapes=[pltpu.VMEM((tc + 2 * HALO, UNIT), F32)],
        compiler_params=_cparams(3),
        name="conv_silu",
    )(z32, z32, z32, conv_w, conv_b)


def _log_sigmoid(x):
    return jnp.minimum(x, 0.0) - jnp.log(1.0 + jnp.exp(-jnp.abs(x)))


class _Stream:
    pass


def _mlstm_kernel(qf_ref, kf_ref, vf_ref, gcolf_ref, growf_ref,
                  qb_ref, kb_ref, vb_ref, gcolb_ref, growb_ref,
                  bcol_ref, brow_ref, hf_ref, hb_ref, c_ref, n_ref, m_ref):
    @pl.when(pl.program_id(1) == 0)
    def _():
        c_ref[...] = jnp.zeros_like(c_ref)
        n_ref[...] = jnp.zeros_like(n_ref)
        m_ref[...] = jnp.zeros_like(m_ref)

    L = ML_CHUNK
    dh = ML_HEAD_DIM
    t_idx = lax.broadcasted_iota(jnp.int32, (L, L), 0)
    s_idx = lax.broadcasted_iota(jnp.int32, (L, L), 1)
    per_direction = (
        (qf_ref, kf_ref, vf_ref, gcolf_ref, growf_ref, hf_ref, s_idx <= t_idx, t_idx <= s_idx),
        (qb_ref, kb_ref, vb_ref, gcolb_ref, growb_ref, hb_ref, s_idx >= t_idx, t_idx >= s_idx),
    )

    streams = []
    for direction, (q_ref, k_ref, v_ref, gcol_ref, grow_ref, h_ref, seen, seen_t) in enumerate(per_direction):
        seen_f = seen.astype(F32)
        seen_t_f = seen_t.astype(F32)
        g_col = gcol_ref[...] + bcol_ref[direction]
        g_row = grow_ref[...] + brow_ref[direction]
        for h in range(ML_HEADS):
            sl = slice(h * dh, (h + 1) * dh)
            s = _Stream()
            s.state, s.sl, s.h_ref, s.seen = direction * ML_HEADS + h, sl, h_ref, seen
            s.q, s.k, s.v = q_ref[:, sl], k_ref[:, sl], v_ref[:, sl]
            s.i_col = g_col[:, h:h + 1]
            s.i_row = g_row[h:h + 1, :]
            f_col = _log_sigmoid(g_col[:, ML_HEADS + h:ML_HEADS + h + 1])
            f_row = _log_sigmoid(g_row[ML_HEADS + h:ML_HEADS + h + 1, :])
            s.b_col = jnp.sum(seen_f * f_row, axis=1, keepdims=True)
            s.b_row = jnp.sum(seen_t_f * f_col, axis=0, keepdims=True)
            s.b_last = jnp.sum(f_row, axis=1, keepdims=True)
            streams.append(s)
    for s in streams:
        s.dmat = jnp.where(s.seen, s.b_col - s.b_row + s.i_row, NEG_BIG)
        s.dmax = jnp.max(s.dmat, axis=1, keepdims=True)
    for s in streams:
        s.m_prev = m_ref[s.state]
        s.c_prev = c_ref[s.state]
        s.n_prev = n_ref[s.state]
        s.qk = lax.dot_general(s.q, s.k, (((1,), (1,)), ((), ())), preferred_element_type=F32)
        s.qc = jnp.dot(s.q, s.c_prev.astype(BF16), preferred_element_type=F32)
    for s in streams:
        inter = s.b_col + s.m_prev
        s.mt = jnp.maximum(inter, s.dmax)
        s.sc = s.qk * jnp.exp(s.dmat - s.mt)
        s.a_inter = jnp.exp(inter - s.mt)
    for s in streams:
        s.num = jnp.dot(s.sc.astype(BF16), s.v, preferred_element_type=F32) + s.a_inter * s.qc
        s.den = (jnp.sum(s.sc, axis=1, keepdims=True)
                 + s.a_inter * jnp.sum(s.q.astype(F32) * s.n_prev, axis=1, keepdims=True))
    for s in streams:
        s.h_ref[:, s.sl] = s.num / jnp.maximum(jnp.abs(s.den), jnp.exp(-s.mt))
    for s in streams:
        g_end = s.b_last - s.b_col + s.i_col
        s.m_new = jnp.maximum(s.b_last + s.m_prev, jnp.max(g_end, axis=0, keepdims=True))
        s.wk = s.k.astype(F32) * jnp.exp(g_end - s.m_new)
        s.decay = jnp.exp(s.b_last + s.m_prev - s.m_new)
    for s in streams:
        c_ref[s.state] = s.decay * s.c_prev + lax.dot_general(
            s.wk.astype(BF16), s.v, (((0,), (0,)), ((), ())), preferred_element_type=F32)
        n_ref[s.state] = s.decay * s.n_prev + jnp.sum(s.wk, axis=0, keepdims=True)
        m_ref[s.state] = s.m_new


def _mlstm(qk, z16, gates_col, gates_row, bias_col, bias_row, *, batch, seq):
    assert seq % ML_CHUNK == 0
    nc = seq // ML_CHUNK
    L = ML_CHUNK
    n_state = 2 * ML_HEADS
    fwd = lambda b, c: b * nc + c
    bwd = lambda b, c: b * nc + nc - 1 - c

    def chunk_specs(direction, tok):
        return [
            pl.BlockSpec((L, ML_WIDTH), lambda b, c: (tok(b, c), 0)),
            pl.BlockSpec((L, ML_WIDTH), lambda b, c: (tok(b, c), 1)),
            pl.BlockSpec((L, UNIT), lambda b, c: (tok(b, c), U16_V_ML)),
            pl.BlockSpec((None, L, 2 * ML_HEADS), lambda b, c: (direction, tok(b, c), 0)),
            pl.BlockSpec((None, 2 * ML_HEADS, L), lambda b, c: (direction, 0, tok(b, c))),
        ]

    out_sds = jax.ShapeDtypeStruct((batch * seq, ML_WIDTH), F32)
    return pl.pallas_call(
        _mlstm_kernel,
        grid=(batch, nc),
        in_specs=chunk_specs(0, fwd) + chunk_specs(1, bwd) + [
            pl.BlockSpec((2, 1, 2 * ML_HEADS), lambda b, c: (0, 0, 0)),
            pl.BlockSpec((2, 2 * ML_HEADS, 1), lambda b, c: (0, 0, 0)),
        ],
        out_specs=[
            pl.BlockSpec((L, ML_WIDTH), lambda b, c: (fwd(b, c), 0)),
            pl.BlockSpec((L, ML_WIDTH), lambda b, c: (bwd(b, c), 0)),
        ],
        out_shape=[out_sds, out_sds],
        scratch_shapes=[
            pltpu.VMEM((n_state, ML_HEAD_DIM, ML_HEAD_DIM), F32),
            pltpu.VMEM((n_state, 1, ML_HEAD_DIM), F32),
            pltpu.VMEM((n_state, 1, 1), F32),
        ],
        compiler_params=_cparams(2),
        name="mlstm",
    )(qk, qk, z16, gates_col, gates_row, qk, qk, z16, gates_col, gates_row, bias_col, bias_row)


def _merge_ln_kernel(x_ref, hf_ref, hb_ref, o_ref, ya_ref, ga_ref, gm_ref, ng_ref,
                     wpa_ref, wpm_ref, wout_ref, g_ref, b_ref, out_ref):
    h = hf_ref[...] + hb_ref[...]
    parts = []
    for hh in range(ML_HEADS):
        sl = slice(hh * ML_HEAD_DIM, (hh + 1) * ML_HEAD_DIM)
        hs = h[:, sl]
        mu = jnp.mean(hs, axis=-1, keepdims=True)
        hc = hs - mu
        var = jnp.mean(hc * hc, axis=-1, keepdims=True)
        parts.append(hc * lax.rsqrt(var + LN_EPS))
    y_m = jnp.concatenate(parts, axis=-1) * ng_ref[...] * _sigmoid(o_ref[...])
    pa = jnp.dot(ya_ref[...], wpa_ref[...], preferred_element_type=F32)
    pm = jnp.dot(y_m.astype(BF16), wpm_ref[...], preferred_element_type=F32)
    merged = _sigmoid(ga_ref[...]) * pa + _sigmoid(gm_ref[...]) * pm
    mix = jnp.dot(merged.astype(BF16), wout_ref[...], preferred_element_type=F32)
    out_ref[...] = _layernorm_rows(ALPHA * x_ref[...] + mix, g_ref[...], b_ref[...])


def _merge_ln(x, h_fwd, h_bwd, z32, y_a, norm_g, w_pa, w_pm, w_out, g, b, *, tm=256):
    t, d = x.shape
    assert t % tm == 0 and U32_G_NA % 2 == 0 and U32_G_ML % 2 == 0
    wide = 2 * UNIT
    const = lambda i: (0, 0)
    return pl.pallas_call(
        _merge_ln_kernel,
        grid=(t // tm,),
        in_specs=[
            pl.BlockSpec((tm, d), lambda i: (i, 0)),
            pl.BlockSpec((tm, ML_WIDTH), lambda i: (i, 0)),
            pl.BlockSpec((tm, ML_WIDTH), lambda i: (i, 0)),
            pl.BlockSpec((tm, UNIT), lambda i: (i, U32_O_ML)),
            pl.BlockSpec((tm, NA_WIDTH), lambda i: (i, 0)),
            pl.BlockSpec((tm, wide), lambda i: (i, U32_G_NA // 2)),
            pl.BlockSpec((tm, wide), lambda i: (i, U32_G_ML // 2)),
            pl.BlockSpec((1, ML_WIDTH), const),
            pl.BlockSpec((NA_WIDTH, d), const, pipeline_mode=pl.Buffered(1)),
            pl.BlockSpec((ML_WIDTH, d), const, pipeline_mode=pl.Buffered(1)),
            pl.BlockSpec((d, d), const, pipeline_mode=pl.Buffered(1)),
            pl.BlockSpec((1, d), const),
            pl.BlockSpec((1, d), const),
        ],
        out_specs=pl.BlockSpec((tm, d), lambda i: (i, 0)),
        out_shape=jax.ShapeDtypeStruct((t, d), F32),
        compiler_params=_cparams(1),
        name="merge_ln",
    )(x, h_fwd, h_bwd, z32, y_a, z32, z32, norm_g, w_pa, w_pm, w_out, g, b)


def _block_units(w):
    d, n = w.shape
    return w.reshape(d, n // UNIT, UNIT).transpose(1, 0, 2)


def _prepare_params(ffa_w_gu, ffa_w_down, norm_a_g, norm_a_b, mix_w_in, na_rpb, ml_conv_w, ml_conv_b,
                    ml_gate_b, ml_norm_g, mix_w_pa, mix_w_pm, mix_w_out, norm_m_g, norm_m_b,
                    ffb_w_gu, ffb_w_down, norm_b_g, norm_b_b, l):
    sizes = (NA_WIDTH, NA_WIDTH, NA_WIDTH, 2 * ML_WIDTH, ML_WIDTH, ML_WIDTH, N_GATE, D_MODEL, D_MODEL)
    points = np.cumsum(sizes)[:-1].tolist()
    w_qa, w_ka, w_va, w_qkm, w_vm, w_om, w_gates, w_ga, w_gm = jnp.split(
        mix_w_in[l].astype(BF16), points, axis=1)
    w_in32 = _block_units(jnp.concatenate([w_ga, w_gm, w_qkm, w_om], axis=1))
    w_in16 = _block_units(jnp.concatenate([w_qa, w_ka, w_va, w_vm], axis=1))
    assert w_in32.shape[0] == N_UNITS32 and w_in16.shape[0] == N_UNITS16
    w_gate = jnp.pad(w_gates, ((0, 0), (0, GATE_PAD - N_GATE)))
    gb = ml_gate_b[l].astype(F32).reshape(2, 2, ML_HEADS).transpose(1, 0, 2).reshape(2, 2 * ML_HEADS)
    row = lambda v: v.reshape(1, -1).astype(F32)
    return dict(
        ffa_w_gu=_block_gu(ffa_w_gu[l].astype(BF16), FFN_TF), ffa_w_down=ffa_w_down[l].astype(BF16),
        norm_a_g=row(norm_a_g[l]), norm_a_b=row(norm_a_b[l]),
        w_in32=w_in32, w_in16=w_in16, w_gate=w_gate,
        na_tables=_na_bias_tables(na_rpb[l]),
        conv_w=ml_conv_w[l].astype(F32), conv_b=row(ml_conv_b[l]),
        gate_b_col=gb.reshape(2, 1, 2 * ML_HEADS), gate_b_row=gb.reshape(2, 2 * ML_HEADS, 1),
        ml_norm_g=row(ml_norm_g[l]),
        w_pa=mix_w_pa[l].astype(BF16), w_pm=mix_w_pm[l].astype(BF16), w_out=mix_w_out[l].astype(BF16),
        norm_m_g=row(norm_m_g[l]), norm_m_b=row(norm_m_b[l]),
        ffb_w_gu=_block_gu(ffb_w_gu[l].astype(BF16), FFN_TF), ffb_w_down=ffb_w_down[l].astype(BF16),
        norm_b_g=row(norm_b_g[l]), norm_b_b=row(norm_b_b[l]),
    )


def _encoder_layer(x, p):
    batch, seq, d = x.shape
    t = batch * seq
    x0 = x.reshape(t, d)
    x1 = _ffn_ln(x0, p["ffa_w_gu"], p["ffa_w_down"], p["norm_a_g"], p["norm_a_b"])
    z32, gates = _in_proj32(x1, p["w_in32"], p["w_gate"])
    z16 = _in_proj16(x1, p["w_in16"])
    y_a = _neighbourhood_attention(z16, p["na_tables"], batch=batch, seq=seq)
    qk = _conv_silu(z32, p["conv_w"], p["conv_b"], batch=batch, seq=seq)
    gd = gates[:, :N_GATE].reshape(t, 2, 2, ML_HEADS)
    gates_col = gd.transpose(2, 0, 1, 3).reshape(2, t, 2 * ML_HEADS)
    gates_row = gd.transpose(2, 1, 3, 0).reshape(2, 2 * ML_HEADS, t)
    h_fwd, h_bwd = _mlstm(qk, z16, gates_col, gates_row, p["gate_b_col"], p["gate_b_row"],
                          batch=batch, seq=seq)
    x2 = _merge_ln(x1, h_fwd, h_bwd, z32, y_a, p["ml_norm_g"], p["w_pa"], p["w_pm"], p["w_out"],
                   p["norm_m_g"], p["norm_m_b"])
    y = _ffn_ln(x2, p["ffb_w_gu"], p["ffb_w_down"], p["norm_b_g"], p["norm_b_b"])
    return y.reshape(batch, seq, d)


def kernel(x_prompt, x_sample, ffa_w_gu, ffa_w_down, norm_a_g, norm_a_b, mix_w_in, na_rpb, ml_conv_w, ml_conv_b, ml_gate_b, ml_norm_g, mix_w_pa, mix_w_pm, mix_w_out, norm_m_g, norm_m_b, ffb_w_gu, ffb_w_down, norm_b_g, norm_b_b):
    assert ffa_w_gu.shape[0] == DEPTH
    y_prompt, y_sample = x_prompt, x_sample
    for l in range(DEPTH):
        p = _prepare_params(ffa_w_gu, ffa_w_down, norm_a_g, norm_a_b, mix_w_in, na_rpb, ml_conv_w,
                            ml_conv_b, ml_gate_b, ml_norm_g, mix_w_pa, mix_w_pm, mix_w_out, norm_m_g,
                            norm_m_b, ffb_w_gu, ffb_w_down, norm_b_g, norm_b_b, l)
        y_prompt = _encoder_layer(y_prompt, p)
        y_sample = _encoder_layer(y_sample, p)
    return (y_prompt, y_sample)
```

```python
import functools

import jax
import jax.numpy as jnp
import numpy as np
from jax import lax
from jax.experimental import pallas as pl
from jax.experimental.pallas import tpu as pltpu

D_MODEL = 2048
D_FF = 5632
DEPTH = 1
ALPHA = (2 * DEPTH) ** 0.25
LN_EPS = 1e-5

GRID_W = 64
NA_HEADS = 8
NA_HEAD_DIM = 128
NA_WIDTH = NA_HEADS * NA_HEAD_DIM
NA_KH = 8
NA_KW = 16
NA_QROWS = 4
NA_KROWS = 12
NA_BLOCKS_PER_STEP = 4
NA_RPB_ROWS = 2 * NA_KH - 1
NA_RPB_COLS = 2 * NA_KW - 1

ML_HEADS = 4
ML_HEAD_DIM = 256
ML_WIDTH = ML_HEADS * ML_HEAD_DIM
ML_CHUNK = 128
ML_CONV = 5
N_GATE = 4 * ML_HEADS

UNIT = 1024
U32_G_NA, U32_G_ML, U32_Q_ML, U32_K_ML, U32_O_ML = 0, 2, 4, 5, 6
N_UNITS32 = 7
U16_Q_NA, U16_K_NA, U16_V_NA, U16_V_ML = 0, 1, 2, 3
N_UNITS16 = 4
GATE_PAD = 128

NEG_BIG = -1e30
VMEM_LIMIT = 56 * 1024 * 1024

F32 = jnp.float32
BF16 = jnp.bfloat16


def _cparams(n_grid_dims):
    return pltpu.CompilerParams(dimension_semantics=("arbitrary",) * n_grid_dims,
                                vmem_limit_bytes=VMEM_LIMIT)


def _sigmoid(x):
    return 1.0 / (1.0 + jnp.exp(-x))


def _layernorm_rows(y, g, b):
    mu = jnp.mean(y, axis=-1, keepdims=True)
    yc = y - mu
    var = jnp.mean(yc * yc, axis=-1, keepdims=True)
    return yc * lax.rsqrt(var + LN_EPS) * g + b


def _ffn_ln_kernel(x_ref, wg_ref, wu_ref, wd_ref, g_ref, b_ref, o_ref, xb_ref):
    j = pl.program_id(1)

    @pl.when(j == 0)
    def _():
        xb_ref[...] = x_ref[...].astype(BF16)
        o_ref[...] = jnp.zeros_like(o_ref)

    xb = xb_ref[...]
    gate = jnp.dot(xb, wg_ref[...], preferred_element_type=F32)
    up = jnp.dot(xb, wu_ref[...], preferred_element_type=F32)
    act = gate * _sigmoid(gate) * up
    o_ref[...] += jnp.dot(act.astype(BF16), wd_ref[...], preferred_element_type=F32)

    @pl.when(j == pl.num_programs(1) - 1)
    def _():
        y = ALPHA * x_ref[...] + 0.5 * o_ref[...]
        o_ref[...] = _layernorm_rows(y, g_ref[...], b_ref[...])


FFN_TM = 1024
FFN_TF = 256


def _ffn_ln(x, w_gu, w_down, g, b):
    t, d = x.shape
    d_ff = w_down.shape[0]
    tm, tf = FFN_TM, FFN_TF
    n_ff = d_ff // tf
    assert t % tm == 0 and d_ff % tf == 0 and w_gu.shape == (d, 2 * d_ff)
    return pl.pallas_call(
        _ffn_ln_kernel,
        grid=(t // tm, n_ff),
        in_specs=[
            pl.BlockSpec((tm, d), lambda i, j: (i, 0)),
            pl.BlockSpec((d, tf), lambda i, j: (0, j)),
            pl.BlockSpec((d, tf), lambda i, j: (0, j + n_ff)),
            pl.BlockSpec((tf, d), lambda i, j: (j, 0)),
            pl.BlockSpec((1, d), lambda i, j: (0, 0)),
            pl.BlockSpec((1, d), lambda i, j: (0, 0)),
        ],
        out_specs=pl.BlockSpec((tm, d), lambda i, j: (i, 0)),
        out_shape=jax.ShapeDtypeStruct((t, d), F32),
        scratch_shapes=[pltpu.VMEM((tm, d), BF16)],
        compiler_params=_cparams(2),
        name="ffn_ln",
    )(x, w_gu, w_gu, w_down, g, b)


PROJ_TM = 1024


def _in_proj32_kernel(x_ref, w_ref, wgate_ref, z_ref, gates_ref, xb_ref):
    @pl.when(pl.program_id(1) == 0)
    def _():
        xb_ref[...] = x_ref[...].astype(BF16)
        gates_ref[...] = jnp.dot(xb_ref[...], wgate_ref[...], preferred_element_type=F32)

    z_ref[...] = jnp.dot(xb_ref[...], w_ref[...], preferred_element_type=F32)


def _in_proj32(x, w, w_gate):
    t, d = x.shape
    n_units = w.shape[1] // UNIT
    tm = PROJ_TM
    assert t % tm == 0 and w.shape[1] % UNIT == 0
    return pl.pallas_call(
        _in_proj32_kernel,
        grid=(t // tm, n_units),
        in_specs=[
            pl.BlockSpec((tm, d), lambda i, j: (i, 0)),
            pl.BlockSpec((d, UNIT), lambda i, j: (0, j)),
            pl.BlockSpec((d, GATE_PAD), lambda i, j: (0, 0)),
        ],
        out_specs=[
            pl.BlockSpec((tm, UNIT), lambda i, j: (i, j)),
            pl.BlockSpec((tm, GATE_PAD), lambda i, j: (i, 0)),
        ],
        out_shape=[jax.ShapeDtypeStruct((t, n_units * UNIT), F32),
                   jax.ShapeDtypeStruct((t, GATE_PAD), F32)],
        scratch_shapes=[pltpu.VMEM((tm, d), BF16)],
        compiler_params=_cparams(2),
        name="in_proj32",
    )(x, w, w_gate)


def _in_proj16_kernel(x_ref, w_ref, z_ref, xb_ref):
    j = pl.program_id(1)

    @pl.when(j == 0)
    def _():
        xb_ref[...] = x_ref[...].astype(BF16)

    scale = jnp.where(j == U16_Q_NA, NA_HEAD_DIM ** -0.5, 1.0)
    z = jnp.dot(xb_ref[...], w_ref[...], preferred_element_type=F32)
    z_ref[...] = (z * scale).astype(z_ref.dtype)


def _in_proj16(x, w):
    t, d = x.shape
    n_units = w.shape[1] // UNIT
    tm = PROJ_TM
    assert t % tm == 0 and w.shape[1] % UNIT == 0
    return pl.pallas_call(
        _in_proj16_kernel,
        grid=(t // tm, n_units),
        in_specs=[
            pl.BlockSpec((tm, d), lambda i, j: (i, 0)),
            pl.BlockSpec((d, UNIT), lambda i, j: (0, j)),
        ],
        out_specs=pl.BlockSpec((tm, UNIT), lambda i, j: (i, j)),
        out_shape=jax.ShapeDtypeStruct((t, n_units * UNIT), BF16),
        scratch_shapes=[pltpu.VMEM((tm, d), BF16)],
        compiler_params=_cparams(2),
        name="in_proj16",
    )(x, w)


def _na_bias_tables(rpb):
    heads = rpb.shape[0]
    c = np.arange(GRID_W)[:, None]
    kc = np.arange(GRID_W)[None, :]
    ws = np.clip(c - NA_KW // 2, 0, GRID_W - NA_KW)
    col_ok = (kc >= ws) & (kc < ws + NA_KW)
    period = 2 * GRID_W
    padded = jnp.pad(rpb.astype(F32), ((0, 0), (0, 0), (0, period - NA_RPB_COLS)))
    flat = jnp.broadcast_to(padded[:, :, None, :], (heads, NA_RPB_ROWS, GRID_W, period))
    flat = flat.reshape(heads, NA_RPB_ROWS, GRID_W * period)
    skew = flat[:, :, NA_KW - 1:NA_KW - 1 + GRID_W * (period - 1)]
    colb = skew.reshape(heads, NA_RPB_ROWS, GRID_W, period - 1)[..., :GRID_W]
    colb = jnp.where(jnp.asarray(col_ok)[None, None], colb, NEG_BIG)
    colb = colb.transpose(0, 2, 1, 3)

    def masked(n_rows):
        return jnp.full((heads, GRID_W, n_rows, GRID_W), NEG_BIG, F32)

    tables = []
    for ty in range(3):
        per_query_row = []
        for a in range(NA_QROWS):
            if ty == 0:
                start, q_row = 0, a
            elif ty == 1:
                start, q_row = a, a + NA_KH // 2
            else:
                start, q_row = NA_KROWS - NA_KH, a + NA_KROWS - NA_QROWS
            first_rpb_row = start - q_row + NA_KH - 1
            assert 0 <= first_rpb_row and first_rpb_row + NA_KH <= NA_RPB_ROWS
            per_query_row.append(jnp.concatenate(
                [masked(start), colb[:, :, first_rpb_row:first_rpb_row + NA_KH],
                 masked(NA_KROWS - NA_KH - start)], axis=2))
        tab = jnp.stack(per_query_row, axis=1)
        tables.append(tab.reshape(heads, NA_QROWS * GRID_W, NA_KROWS * GRID_W))
    return jnp.stack(tables)


def _na_kernel(q_ref, k_ref, v_ref, tab_first_ref, tab_mid_ref, tab_last_ref, o_ref, *, rows):
    i = pl.program_id(2)
    nq = NA_QROWS * GRID_W
    nk = NA_KROWS * GRID_W
    tab_refs = (tab_first_ref,) + (tab_mid_ref,) * (NA_BLOCKS_PER_STEP - 2) + (tab_last_ref,)
    starts, scores = [], []
    for blk in range(NA_BLOCKS_PER_STEP):
        q_row0 = NA_QROWS * (NA_BLOCKS_PER_STEP * i + blk)
        first_key_row = jnp.clip(q_row0 - NA_KH // 2, 0, rows - NA_KROWS)
        start = pl.multiple_of(first_key_row * GRID_W, GRID_W * NA_QROWS)
        q = q_ref[blk * nq:(blk + 1) * nq, :]
        k = k_ref[pl.ds(start, nk), :]
        s = lax.dot_general(q, k, (((1,), (1,)), ((), ())), preferred_element_type=F32)
        starts.append(start)
        scores.append(s + tab_refs[blk][...])
    probs, sums = [], []
    for s in scores:
        m = jnp.max(s, axis=-1, keepdims=True)
        p = jnp.exp(s - m)
        sums.append(jnp.sum(p, axis=-1, keepdims=True))
        probs.append(p.astype(BF16))
    for blk in range(NA_BLOCKS_PER_STEP):
        v = v_ref[pl.ds(starts[blk], nk), :]
        o = jnp.dot(probs[blk], v, preferred_element_type=F32)
        o_ref[blk * nq:(blk + 1) * nq, :] = (o / sums[blk]).astype(o_ref.dtype)


def _neighbourhood_attention(z16, tables, *, batch, seq):
    rows = seq // GRID_W
    step_rows = NA_QROWS * NA_BLOCKS_PER_STEP
    assert NA_BLOCKS_PER_STEP >= 2 and NA_KROWS == NA_KH + NA_QROWS
    assert seq % GRID_W == 0 and rows % step_rows == 0 and rows >= 2 * NA_KROWS
    nstep = rows // step_rows
    nq_step = step_rows * GRID_W
    hd = NA_HEAD_DIM
    per_unit = UNIT // hd
    tab_shape = (None, None, NA_QROWS * GRID_W, NA_KROWS * GRID_W)

    return pl.pallas_call(
        functools.partial(_na_kernel, rows=rows),
        grid=(batch, NA_HEADS, nstep),
        in_specs=[
            pl.BlockSpec((nq_step, hd), lambda b, h, i: (b * nstep + i, U16_Q_NA * per_unit + h)),
            pl.BlockSpec((seq, hd), lambda b, h, i: (b, U16_K_NA * per_unit + h)),
            pl.BlockSpec((seq, hd), lambda b, h, i: (b, U16_V_NA * per_unit + h)),
            pl.BlockSpec(tab_shape, lambda b, h, i: (jnp.where(i == 0, 0, 1), h, 0, 0)),
            pl.BlockSpec(tab_shape, lambda b, h, i: (1, h, 0, 0)),
            pl.BlockSpec(tab_shape, lambda b, h, i: (jnp.where(i == nstep - 1, 2, 1), h, 0, 0)),
        ],
        out_specs=pl.BlockSpec((nq_step, hd), lambda b, h, i: (b * nstep + i, h)),
        out_shape=jax.ShapeDtypeStruct((batch * seq, NA_WIDTH), BF16),
        compiler_params=_cparams(3),
        name="na",
    )(z16, z16, z16, tables, tables, tables)


HALO = 8


def _conv_silu_kernel(prev_ref, cur_ref, next_ref, w_ref, b_ref, o_ref, ext_ref, *, tc):
    t = pl.program_id(1)
    part = pl.program_id(2)
    prev_keep = jnp.where(t == 0, 0.0, 1.0)
    next_keep = jnp.where(t == pl.num_programs(1) - 1, 0.0, 1.0)
    ext_ref[0:HALO, :] = prev_ref[...] * prev_keep
    ext_ref[HALO:HALO + tc, :] = cur_ref[...]
    ext_ref[HALO + tc:, :] = next_ref[...] * next_keep
    pad = ML_CONV // 2
    y = jnp.zeros((tc, cur_ref.shape[1]), F32) + b_ref[...]
    for j in range(ML_CONV):
        y = y + ext_ref[pl.ds(HALO - pad + j, tc), :] * w_ref[j:j + 1, :]
    y = y * _sigmoid(y)
    scale = jnp.where(part == 1, ML_HEAD_DIM ** -0.5, 1.0)
    o_ref[...] = (y * scale).astype(o_ref.dtype)


def _conv_silu(z32, conv_w, conv_b, *, batch, seq, tc=512):
    assert seq % tc == 0 and tc % HALO == 0 and U32_K_ML == U32_Q_ML + 1
    nt = seq // tc
    per = tc // HALO
    last_halo = batch * seq // HALO - 1

    return pl.pallas_call(
        functools.partial(_conv_silu_kernel, tc=tc),
        grid=(batch, nt, 2),
        in_specs=[
            pl.BlockSpec((HALO, UNIT), lambda b, t, p: (jnp.maximum((b * nt + t) * per - 1, 0), U32_Q_ML + p)),
            pl.BlockSpec((tc, UNIT), lambda b, t, p: (b * nt + t, U32_Q_ML + p)),
            pl.BlockSpec((HALO, UNIT), lambda b, t, p: (jnp.minimum((b * nt + t + 1) * per, last_halo), U32_Q_ML + p)),
            pl.BlockSpec((ML_CONV, UNIT), lambda b, t, p: (0, p)),
            pl.BlockSpec((1, UNIT), lambda b, t, p: (0, p)),
        ],
        out_specs=pl.BlockSpec((tc, UNIT), lambda b, t, p: (b * nt + t, p)),
        out_shape=jax.ShapeDtypeStruct((batch * seq, 2 * ML_WIDTH), BF16),
        scratch_shapes=[pltpu.VMEM((tc + 2 * HALO, UNIT), F32)],
        compiler_params=_cparams(3),
        name="conv_silu",
    )(z32, z32, z32, conv_w, conv_b)


def _log_sigmoid(x):
    return jnp.minimum(x, 0.0) - jnp.log(1.0 + jnp.exp(-jnp.abs(x)))


class _Stream:
    pass


def _mlstm_kernel(qf_ref, kf_ref, vf_ref, gcolf_ref, growf_ref,
                  qb_ref, kb_ref, vb_ref, gcolb_ref, growb_ref,
                  bcol_ref, brow_ref, hf_ref, hb_ref, c_ref, n_ref, m_ref):
    @pl.when(pl.program_id(1) == 0)
    def _():
        c_ref[...] = jnp.zeros_like(c_ref)
        n_ref[...] = jnp.zeros_like(n_ref)
        m_ref[...] = jnp.zeros_like(m_ref)

    L = ML_CHUNK
    dh = ML_HEAD_DIM
    t_idx = lax.broadcasted_iota(jnp.int32, (L, L), 0)
    s_idx = lax.broadcasted_iota(jnp.int32, (L, L), 1)
    per_direction = (
        (qf_ref, kf_ref, vf_ref, gcolf_ref, growf_ref, hf_ref, s_idx <= t_idx, t_idx <= s_idx),
        (qb_ref, kb_ref, vb_ref, gcolb_ref, growb_ref, hb_ref, s_idx >= t_idx, t_idx >= s_idx),
    )

    streams = []
    for direction, (q_ref, k_ref, v_ref, gcol_ref, grow_ref, h_ref, seen, seen_t) in enumerate(per_direction):
        seen_f = seen.astype(F32)
        seen_t_f = seen_t.astype(F32)
        g_col = gcol_ref[...] + bcol_ref[direction]
        g_row = grow_ref[...] + brow_ref[direction]
        for h in range(ML_HEADS):
            sl = slice(h * dh, (h + 1) * dh)
            s = _Stream()
            s.state, s.sl, s.h_ref, s.seen = direction * ML_HEADS + h, sl, h_ref, seen
            s.q, s.k, s.v = q_ref[:, sl], k_ref[:, sl], v_ref[:, sl]
            s.i_col = g_col[:, h:h + 1]
            s.i_row = g_row[h:h + 1, :]
            f_col = _log_sigmoid(g_col[:, ML_HEADS + h:ML_HEADS + h + 1])
            f_row = _log_sigmoid(g_row[ML_HEADS + h:ML_HEADS + h + 1, :])
            s.b_col = jnp.sum(seen_f * f_row, axis=1, keepdims=True)
            s.b_row = jnp.sum(seen_t_f * f_col, axis=0, keepdims=True)
            s.b_last = jnp.sum(f_row, axis=1, keepdims=True)
            streams.append(s)
    for s in streams:
        s.dmat = jnp.where(s.seen, s.b_col - s.b_row + s.i_row, NEG_BIG)
        s.dmax = jnp.max(s.dmat, axis=1, keepdims=True)
    for s in streams:
        s.m_prev = m_ref[s.state]
        s.c_prev = c_ref[s.state]
        s.n_prev = n_ref[s.state]
        s.qk = lax.dot_general(s.q, s.k, (((1,), (1,)), ((), ())), preferred_element_type=F32)
        s.qc = jnp.dot(s.q, s.c_prev.astype(BF16), preferred_element_type=F32)
    for s in streams:
        inter = s.b_col + s.m_prev
        s.mt = jnp.maximum(inter, s.dmax)
        s.sc = s.qk * jnp.exp(s.dmat - s.mt)
        s.a_inter = jnp.exp(inter - s.mt)
    for s in streams:
        s.num = jnp.dot(s.sc.astype(BF16), s.v, preferred_element_type=F32) + s.a_inter * s.qc
        s.den = (jnp.sum(s.sc, axis=1, keepdims=True)
                 + s.a_inter * jnp.sum(s.q.astype(F32) * s.n_prev, axis=1, keepdims=True))
    for s in streams:
        s.h_ref[:, s.sl] = s.num / jnp.maximum(jnp.abs(s.den), jnp.exp(-s.mt))
    for s in streams:
        g_end = s.b_last - s.b_col + s.i_col
        s.m_new = jnp.maximum(s.b_last + s.m_prev, jnp.max(g_end, axis=0, keepdims=True))
        s.wk = s.k.astype(F32) * jnp.exp(g_end - s.m_new)
        s.decay = jnp.exp(s.b_last + s.m_prev - s.m_new)
    for s in streams:
        c_ref[s.state] = s.decay * s.c_prev + lax.dot_general(
            s.wk.astype(BF16), s.v, (((0,), (0,)), ((), ())), preferred_element_type=F32)
        n_ref[s.state] = s.decay * s.n_prev + jnp.sum(s.wk, axis=0, keepdims=True)
        m_ref[s.state] = s.m_new


def _mlstm(qk, z16, gates_col, gates_row, bias_col, bias_row, *, batch, seq):
    assert seq % ML_CHUNK == 0
    nc = seq // ML_CHUNK
    L = ML_CHUNK
    n_state = 2 * ML_HEADS
    fwd = lambda b, c: b * nc + c
    bwd = lambda b, c: b * nc + nc - 1 - c

    def chunk_specs(direction, tok):
        return [
            pl.BlockSpec((L, ML_WIDTH), lambda b, c: (tok(b, c), 0)),
            pl.BlockSpec((L, ML_WIDTH), lambda b, c: (tok(b, c), 1)),
            pl.BlockSpec((L, UNIT), lambda b, c: (tok(b, c), U16_V_ML)),
            pl.BlockSpec((None, L, 2 * ML_HEADS), lambda b, c: (direction, tok(b, c), 0)),
            pl.BlockSpec((None, 2 * ML_HEADS, L), lambda b, c: (direction, 0, tok(b, c))),
        ]

    out_sds = jax.ShapeDtypeStruct((batch * seq, ML_WIDTH), F32)
    return pl.pallas_call(
        _mlstm_kernel,
        grid=(batch, nc),
        in_specs=chunk_specs(0, fwd) + chunk_specs(1, bwd) + [
            pl.BlockSpec((2, 1, 2 * ML_HEADS), lambda b, c: (0, 0, 0)),
            pl.BlockSpec((2, 2 * ML_HEADS, 1), lambda b, c: (0, 0, 0)),
        ],
        out_specs=[
            pl.BlockSpec((L, ML_WIDTH), lambda b, c: (fwd(b, c), 0)),
            pl.BlockSpec((L, ML_WIDTH), lambda b, c: (bwd(b, c), 0)),
        ],
        out_shape=[out_sds, out_sds],
        scratch_shapes=[
            pltpu.VMEM((n_state, ML_HEAD_DIM, ML_HEAD_DIM), F32),
            pltpu.VMEM((n_state, 1, ML_HEAD_DIM), F32),
            pltpu.VMEM((n_state, 1, 1), F32),
        ],
        compiler_params=_cparams(2),
        name="mlstm",
    )(qk, qk, z16, gates_col, gates_row, qk, qk, z16, gates_col, gates_row, bias_col, bias_row)


def _merge_ln_kernel(x_ref, hf_ref, hb_ref, o_ref, ya_ref, ga_ref, gm_ref, ng_ref,
                     wpa_ref, wpm_ref, wout_ref, g_ref, b_ref, out_ref):
    h = hf_ref[...] + hb_ref[...]
    parts = []
    for hh in range(ML_HEADS):
        sl = slice(hh * ML_HEAD_DIM, (hh + 1) * ML_HEAD_DIM)
        hs = h[:, sl]
        mu = jnp.mean(hs, axis=-1, keepdims=True)
        hc = hs - mu
        var = jnp.mean(hc * hc, axis=-1, keepdims=True)
        parts.append(hc * lax.rsqrt(var + LN_EPS))
    y_m = jnp.concatenate(parts, axis=-1) * ng_ref[...] * _sigmoid(o_ref[...])
    pa = jnp.dot(ya_ref[...], wpa_ref[...], preferred_element_type=F32)
    pm = jnp.dot(y_m.astype(BF16), wpm_ref[...], preferred_element_type=F32)
    merged = _sigmoid(ga_ref[...]) * pa + _sigmoid(gm_ref[...]) * pm
    mix = jnp.dot(merged.astype(BF16), wout_ref[...], preferred_element_type=F32)
    out_ref[...] = _layernorm_rows(ALPHA * x_ref[...] + mix, g_ref[...], b_ref[...])


def _merge_ln(x, h_fwd, h_bwd, z32, y_a, norm_g, w_pa, w_pm, w_out, g, b, *, tm=256):
    t, d = x.shape
    assert t % tm == 0 and U32_G_NA % 2 == 0 and U32_G_ML % 2 == 0
    wide = 2 * UNIT
    const = lambda i: (0, 0)
    return pl.pallas_call(
        _merge_ln_kernel,
        grid=(t // tm,),
        in_specs=[
            pl.BlockSpec((tm, d), lambda i: (i, 0)),
            pl.BlockSpec((tm, ML_WIDTH), lambda i: (i, 0)),
            pl.BlockSpec((tm, ML_WIDTH), lambda i: (i, 0)),
            pl.BlockSpec((tm, UNIT), lambda i: (i, U32_O_ML)),
            pl.BlockSpec((tm, NA_WIDTH), lambda i: (i, 0)),
            pl.BlockSpec((tm, wide), lambda i: (i, U32_G_NA // 2)),
            pl.BlockSpec((tm, wide), lambda i: (i, U32_G_ML // 2)),
            pl.BlockSpec((1, ML_WIDTH), const),
            pl.BlockSpec((NA_WIDTH, d), const, pipeline_mode=pl.Buffered(1)),
            pl.BlockSpec((ML_WIDTH, d), const, pipeline_mode=pl.Buffered(1)),
            pl.BlockSpec((d, d), const, pipeline_mode=pl.Buffered(1)),
            pl.BlockSpec((1, d), const),
            pl.BlockSpec((1, d), const),
        ],
        out_specs=pl.BlockSpec((tm, d), lambda i: (i, 0)),
        out_shape=jax.ShapeDtypeStruct((t, d), F32),
        compiler_params=_cparams(1),
        name="merge_ln",
    )(x, h_fwd, h_bwd, z32, y_a, z32, z32, norm_g, w_pa, w_pm, w_out, g, b)


def _prepare_params(ffa_w_gu, ffa_w_down, norm_a_g, norm_a_b, mix_w_in, na_rpb, ml_conv_w, ml_conv_b,
                    ml_gate_b, ml_norm_g, mix_w_pa, mix_w_pm, mix_w_out, norm_m_g, norm_m_b,
                    ffb_w_gu, ffb_w_down, norm_b_g, norm_b_b, l):
    sizes = (NA_WIDTH, NA_WIDTH, NA_WIDTH, 2 * ML_WIDTH, ML_WIDTH, ML_WIDTH, N_GATE, D_MODEL, D_MODEL)
    points = np.cumsum(sizes)[:-1].tolist()
    w_qa, w_ka, w_va, w_qkm, w_vm, w_om, w_gates, w_ga, w_gm = jnp.split(
        mix_w_in[l].astype(BF16), points, axis=1)
    w_in32 = jnp.concatenate([w_ga, w_gm, w_qkm, w_om], axis=1)
    w_in16 = jnp.concatenate([w_qa, w_ka, w_va, w_vm], axis=1)
    assert w_in32.shape[1] == N_UNITS32 * UNIT and w_in16.shape[1] == N_UNITS16 * UNIT
    w_gate = jnp.pad(w_gates, ((0, 0), (0, GATE_PAD - N_GATE)))
    gb = ml_gate_b[l].astype(F32).reshape(2, 2, ML_HEADS).transpose(1, 0, 2).reshape(2, 2 * ML_HEADS)
    row = lambda v: v.reshape(1, -1).astype(F32)
    return dict(
        ffa_w_gu=ffa_w_gu[l].astype(BF16), ffa_w_down=ffa_w_down[l].astype(BF16),
        norm_a_g=row(norm_a_g[l]), norm_a_b=row(norm_a_b[l]),
        w_in32=w_in32, w_in16=w_in16, w_gate=w_gate,
        na_tables=_na_bias_tables(na_rpb[l]),
        conv_w=ml_conv_w[l].astype(F32), conv_b=row(ml_conv_b[l]),
        gate_b_col=gb.reshape(2, 1, 2 * ML_HEADS), gate_b_row=gb.reshape(2, 2 * ML_HEADS, 1),
        ml_norm_g=row(ml_norm_g[l]),
        w_pa=mix_w_pa[l].astype(BF16), w_pm=mix_w_pm[l].astype(BF16), w_out=mix_w_out[l].astype(BF16),
        norm_m_g=row(norm_m_g[l]), norm_m_b=row(norm_m_b[l]),
        ffb_w_gu=ffb_w_gu[l].astype(BF16), ffb_w_down=ffb_w_down[l].astype(BF16),
        norm_b_g=row(norm_b_g[l]), norm_b_b=row(norm_b_b[l]),
    )


def _encoder_layer(x, p):
    batch, seq, d = x.shape
    t = batch * seq
    x0 = x.reshape(t, d)
    x1 = _ffn_ln(x0, p["ffa_w_gu"], p["ffa_w_down"], p["norm_a_g"], p["norm_a_b"])
    z32, gates = _in_proj32(x1, p["w_in32"], p["w_gate"])
    z16 = _in_proj16(x1, p["w_in16"])
    y_a = _neighbourhood_attention(z16, p["na_tables"], batch=batch, seq=seq)
    qk = _conv_silu(z32, p["conv_w"], p["conv_b"], batch=batch, seq=seq)
    gd = gates[:, :N_GATE].reshape(t, 2, 2, ML_HEADS)
    gates_col = gd.transpose(2, 0, 1, 3).reshape(2, t, 2 * ML_HEADS)
    gates_row = gd.transpose(2, 1, 3, 0).reshape(2, 2 * ML_HEADS, t)
    h_fwd, h_bwd = _mlstm(qk, z16, gates_col, gates_row, p["gate_b_col"], p["gate_b_row"],
                          batch=batch, seq=seq)
    x2 = _merge_ln(x1, h_fwd, h_bwd, z32, y_a, p["ml_norm_g"], p["w_pa"], p["w_pm"], p["w_out"],
                   p["norm_m_g"], p["norm_m_b"])
    y = _ffn_ln(x2, p["ffb_w_gu"], p["ffb_w_down"], p["norm_b_g"], p["norm_b_b"])
    return y.reshape(batch, seq, d)


def kernel(x_prompt, x_sample, ffa_w_gu, ffa_w_down, norm_a_g, norm_a_b, mix_w_in, na_rpb, ml_conv_w, ml_conv_b, ml_gate_b, ml_norm_g, mix_w_pa, mix_w_pm, mix_w_out, norm_m_g, norm_m_b, ffb_w_gu, ffb_w_down, norm_b_g, norm_b_b):
    assert ffa_w_gu.shape[0] == DEPTH
    y_prompt, y_sample = x_prompt, x_sample
    for l in range(DEPTH):
        p = _prepare_params(ffa_w_gu, ffa_w_down, norm_a_g, norm_a_b, mix_w_in, na_rpb, ml_conv_w,
                            ml_conv_b, ml_gate_b, ml_norm_g, mix_w_pa, mix_w_pm, mix_w_out, norm_m_g,
                            norm_m_b, ffb_w_gu, ffb_w_down, norm_b_g, norm_b_b, l)
        y_prompt = _encoder_layer(y_prompt, p)
        y_sample = _encoder_layer(y_sample, p)
    return (y_prompt, y_sample)
```

```python
import functools

import jax
import jax.numpy as jnp
import numpy as np
from jax import lax
from jax.experimental import pallas as pl
from jax.experimental.pallas import tpu as pltpu

D_MODEL = 2048
D_FF = 5632
DEPTH = 1
ALPHA = (2 * DEPTH) ** 0.25
LN_EPS = 1e-5

GRID_W = 64
NA_HEADS = 8
NA_HEAD_DIM = 128
NA_WIDTH = NA_HEADS * NA_HEAD_DIM
NA_KH = 8
NA_KW = 16
NA_QROWS = 4
NA_KROWS = 12
NA_BLOCKS_PER_STEP = 4
NA_RPB_ROWS = 2 * NA_KH - 1
NA_RPB_COLS = 2 * NA_KW - 1

ML_HEADS = 4
ML_HEAD_DIM = 256
ML_WIDTH = ML_HEADS * ML_HEAD_DIM
ML_CHUNK = 128
ML_CONV = 5
N_GATE = 4 * ML_HEADS

UNIT = 1024
U_G_NA, U_G_ML, U_Q_NA, U_K_NA, U_V_NA, U_Q_ML, U_K_ML, U_V_ML, U_O_ML = 0, 2, 4, 5, 6, 7, 8, 9, 10
N_UNITS = 11
GATE_PAD = 128

NEG_BIG = -1e30
VMEM_LIMIT = 60 * 1024 * 1024

F32 = jnp.float32
BF16 = jnp.bfloat16


def _cparams(n_grid_dims):
    return pltpu.CompilerParams(dimension_semantics=("arbitrary",) * n_grid_dims,
                                vmem_limit_bytes=VMEM_LIMIT)


def _sigmoid(x):
    return 1.0 / (1.0 + jnp.exp(-x))


def _layernorm_rows(y, g, b):
    mu = jnp.mean(y, axis=-1, keepdims=True)
    yc = y - mu
    var = jnp.mean(yc * yc, axis=-1, keepdims=True)
    return yc * lax.rsqrt(var + LN_EPS) * g + b


def _ffn_ln_kernel(x_ref, wgu_ref, wd_ref, g_ref, b_ref, o_ref, xb_ref, *, tf):
    j = pl.program_id(1)

    @pl.when(j == 0)
    def _():
        xb_ref[...] = x_ref[...].astype(BF16)
        o_ref[...] = jnp.zeros_like(o_ref)

    gu = jnp.dot(xb_ref[...], wgu_ref[...], preferred_element_type=F32)
    gate = gu[:, :tf]
    up = gu[:, tf:]
    act = gate * _sigmoid(gate) * up
    o_ref[...] += jnp.dot(act.astype(BF16), wd_ref[...], preferred_element_type=F32)

    @pl.when(j == pl.num_programs(1) - 1)
    def _():
        y = ALPHA * x_ref[...] + 0.5 * o_ref[...]
        o_ref[...] = _layernorm_rows(y, g_ref[...], b_ref[...])


FFN_TM = 1024
FFN_TF = 256


def _block_gu(w_gu):
    d, two_ff = w_gu.shape
    d_ff = two_ff // 2
    n_ff = d_ff // FFN_TF
    both = w_gu.reshape(d, 2, n_ff, FFN_TF)
    return both.transpose(2, 0, 1, 3).reshape(n_ff, d, 2 * FFN_TF).astype(BF16)


def _ffn_ln(x, w_gu_blocked, w_down, g, b):
    t, d = x.shape
    n_ff, _, two_tf = w_gu_blocked.shape
    tf = two_tf // 2
    tm = FFN_TM
    assert t % tm == 0 and w_down.shape[0] == n_ff * tf
    return pl.pallas_call(
        functools.partial(_ffn_ln_kernel, tf=tf),
        grid=(t // tm, n_ff),
        in_specs=[
            pl.BlockSpec((tm, d), lambda i, j: (i, 0)),
            pl.BlockSpec((None, d, two_tf), lambda i, j: (j, 0, 0)),
            pl.BlockSpec((tf, d), lambda i, j: (j, 0)),
            pl.BlockSpec((1, d), lambda i, j: (0, 0)),
            pl.BlockSpec((1, d), lambda i, j: (0, 0)),
        ],
        out_specs=pl.BlockSpec((tm, d), lambda i, j: (i, 0)),
        out_shape=jax.ShapeDtypeStruct((t, d), F32),
        scratch_shapes=[pltpu.VMEM((tm, d), BF16)],
        compiler_params=_cparams(2),
        name="ffn_ln",
    )(x, w_gu_blocked, w_down, g, b)


PROJ_TM = 1024


def _in_proj_kernel(x_ref, w_ref, wgate_ref, z_ref, gates_ref, xb_ref):
    j = pl.program_id(1)

    @pl.when(j == 0)
    def _():
        xb_ref[...] = x_ref[...].astype(BF16)
        gates_ref[...] = jnp.dot(xb_ref[...], wgate_ref[...], preferred_element_type=F32)

    scale = jnp.where(j == U_Q_NA, NA_HEAD_DIM ** -0.5, 1.0)
    z = jnp.dot(xb_ref[...], w_ref[...], preferred_element_type=F32)
    z_ref[...] = (z * scale).astype(z_ref.dtype)


def _in_proj(x, w, w_gate):
    t, d = x.shape
    n_units = w.shape[1] // UNIT
    tm = PROJ_TM
    assert t % tm == 0 and w.shape[1] % UNIT == 0
    return pl.pallas_call(
        _in_proj_kernel,
        grid=(t // tm, n_units),
        in_specs=[
            pl.BlockSpec((tm, d), lambda i, j: (i, 0)),
            pl.BlockSpec((d, UNIT), lambda i, j: (0, j)),
            pl.BlockSpec((d, GATE_PAD), lambda i, j: (0, 0)),
        ],
        out_specs=[
            pl.BlockSpec((tm, UNIT), lambda i, j: (i, j)),
            pl.BlockSpec((tm, GATE_PAD), lambda i, j: (i, 0)),
        ],
        out_shape=[jax.ShapeDtypeStruct((t, n_units * UNIT), BF16),
                   jax.ShapeDtypeStruct((t, GATE_PAD), F32)],
        scratch_shapes=[pltpu.VMEM((tm, d), BF16)],
        compiler_params=_cparams(2),
        name="in_proj",
    )(x, w, w_gate)


def _na_bias_tables(rpb):
    heads = rpb.shape[0]
    c = np.arange(GRID_W)[:, None]
    kc = np.arange(GRID_W)[None, :]
    ws = np.clip(c - NA_KW // 2, 0, GRID_W - NA_KW)
    col_ok = (kc >= ws) & (kc < ws + NA_KW)
    period = 2 * GRID_W
    padded = jnp.pad(rpb.astype(F32), ((0, 0), (0, 0), (0, period - NA_RPB_COLS)))
    flat = jnp.broadcast_to(padded[:, :, None, :], (heads, NA_RPB_ROWS, GRID_W, period))
    flat = flat.reshape(heads, NA_RPB_ROWS, GRID_W * period)
    skew = flat[:, :, NA_KW - 1:NA_KW - 1 + GRID_W * (period - 1)]
    colb = skew.reshape(heads, NA_RPB_ROWS, GRID_W, period - 1)[..., :GRID_W]
    colb = jnp.where(jnp.asarray(col_ok)[None, None], colb, NEG_BIG)
    colb = colb.transpose(0, 2, 1, 3)

    def masked(n_rows):
        return jnp.full((heads, GRID_W, n_rows, GRID_W), NEG_BIG, F32)

    tables = []
    for ty in range(3):
        per_query_row = []
        for a in range(NA_QROWS):
            if ty == 0:
                start, q_row = 0, a
            elif ty == 1:
                start, q_row = a, a + NA_KH // 2
            else:
                start, q_row = NA_KROWS - NA_KH, a + NA_KROWS - NA_QROWS
            first_rpb_row = start - q_row + NA_KH - 1
            assert 0 <= first_rpb_row and first_rpb_row + NA_KH <= NA_RPB_ROWS
            per_query_row.append(jnp.concatenate(
                [masked(start), colb[:, :, first_rpb_row:first_rpb_row + NA_KH],
                 masked(NA_KROWS - NA_KH - start)], axis=2))
        tab = jnp.stack(per_query_row, axis=1)
        tables.append(tab.reshape(heads, NA_QROWS * GRID_W, NA_KROWS * GRID_W))
    return jnp.stack(tables)


def _na_kernel(q_ref, k_ref, v_ref, tab_first_ref, tab_mid_ref, tab_last_ref, o_ref, *, rows):
    i = pl.program_id(2)
    nq = NA_QROWS * GRID_W
    nk = NA_KROWS * GRID_W
    tab_refs = (tab_first_ref,) + (tab_mid_ref,) * (NA_BLOCKS_PER_STEP - 2) + (tab_last_ref,)
    starts, scores = [], []
    for blk in range(NA_BLOCKS_PER_STEP):
        q_row0 = NA_QROWS * (NA_BLOCKS_PER_STEP * i + blk)
        first_key_row = jnp.clip(q_row0 - NA_KH // 2, 0, rows - NA_KROWS)
        start = pl.multiple_of(first_key_row * GRID_W, GRID_W * NA_QROWS)
        q = q_ref[blk * nq:(blk + 1) * nq, :]
        k = k_ref[pl.ds(start, nk), :]
        s = lax.dot_general(q, k, (((1,), (1,)), ((), ())), preferred_element_type=F32)
        starts.append(start)
        scores.append(s + tab_refs[blk][...])
    probs, sums = [], []
    for s in scores:
        m = jnp.max(s, axis=-1, keepdims=True)
        p = jnp.exp(s - m)
        sums.append(jnp.sum(p, axis=-1, keepdims=True))
        probs.append(p.astype(BF16))
    for blk in range(NA_BLOCKS_PER_STEP):
        v = v_ref[pl.ds(starts[blk], nk), :]
        o = jnp.dot(probs[blk], v, preferred_element_type=F32)
        o_ref[blk * nq:(blk + 1) * nq, :] = (o / sums[blk]).astype(o_ref.dtype)


def _neighbourhood_attention(z, tables, *, batch, seq):
    rows = seq // GRID_W
    step_rows = NA_QROWS * NA_BLOCKS_PER_STEP
    assert NA_BLOCKS_PER_STEP >= 2 and NA_KROWS == NA_KH + NA_QROWS
    assert seq % GRID_W == 0 and rows % step_rows == 0 and rows >= 2 * NA_KROWS
    nstep = rows // step_rows
    nq_step = step_rows * GRID_W
    hd = NA_HEAD_DIM
    per_unit = UNIT // hd
    tab_shape = (None, None, NA_QROWS * GRID_W, NA_KROWS * GRID_W)

    return pl.pallas_call(
        functools.partial(_na_kernel, rows=rows),
        grid=(batch, NA_HEADS, nstep),
        in_specs=[
            pl.BlockSpec((nq_step, hd), lambda b, h, i: (b * nstep + i, U_Q_NA * per_unit + h)),
            pl.BlockSpec((seq, hd), lambda b, h, i: (b, U_K_NA * per_unit + h)),
            pl.BlockSpec((seq, hd), lambda b, h, i: (b, U_V_NA * per_unit + h)),
            pl.BlockSpec(tab_shape, lambda b, h, i: (jnp.where(i == 0, 0, 1), h, 0, 0)),
            pl.BlockSpec(tab_shape, lambda b, h, i: (1, h, 0, 0)),
            pl.BlockSpec(tab_shape, lambda b, h, i: (jnp.where(i == nstep - 1, 2, 1), h, 0, 0)),
        ],
        out_specs=pl.BlockSpec((nq_step, hd), lambda b, h, i: (b * nstep + i, h)),
        out_shape=jax.ShapeDtypeStruct((batch * seq, NA_WIDTH), BF16),
        compiler_params=_cparams(3),
        name="na",
    )(z, z, z, tables, tables, tables)


HALO = 16


def _conv_silu_kernel(prev_ref, cur_ref, next_ref, w_ref, b_ref, o_ref, ext_ref, *, tc):
    t = pl.program_id(1)
    part = pl.program_id(2)
    prev_keep = jnp.where(t == 0, 0.0, 1.0)
    next_keep = jnp.where(t == pl.num_programs(1) - 1, 0.0, 1.0)
    ext_ref[0:HALO, :] = prev_ref[...].astype(F32) * prev_keep
    ext_ref[HALO:HALO + tc, :] = cur_ref[...].astype(F32)
    ext_ref[HALO + tc:, :] = next_ref[...].astype(F32) * next_keep
    pad = ML_CONV // 2
    y = jnp.zeros((tc, cur_ref.shape[1]), F32) + b_ref[...]
    for j in range(ML_CONV):
        y = y + ext_ref[pl.ds(HALO - pad + j, tc), :] * w_ref[j:j + 1, :]
    y = y * _sigmoid(y)
    scale = jnp.where(part == 1, ML_HEAD_DIM ** -0.5, 1.0)
    o_ref[...] = (y * scale).astype(o_ref.dtype)


def _conv_silu(z, conv_w, conv_b, *, batch, seq, tc=512):
    assert seq % tc == 0 and tc % HALO == 0 and U_K_ML == U_Q_ML + 1
    nt = seq // tc
    per = tc // HALO
    last_halo = batch * seq // HALO - 1

    return pl.pallas_call(
        functools.partial(_conv_silu_kernel, tc=tc),
        grid=(batch, nt, 2),
        in_specs=[
            pl.BlockSpec((HALO, UNIT), lambda b, t, p: (jnp.maximum((b * nt + t) * per - 1, 0), U_Q_ML + p)),
            pl.BlockSpec((tc, UNIT), lambda b, t, p: (b * nt + t, U_Q_ML + p)),
            pl.BlockSpec((HALO, UNIT), lambda b, t, p: (jnp.minimum((b * nt + t + 1) * per, last_halo), U_Q_ML + p)),
            pl.BlockSpec((ML_CONV, UNIT), lambda b, t, p: (0, p)),
            pl.BlockSpec((1, UNIT), lambda b, t, p: (0, p)),
        ],
        out_specs=pl.BlockSpec((tc, UNIT), lambda b, t, p: (b * nt + t, p)),
        out_shape=jax.ShapeDtypeStruct((batch * seq, 2 * ML_WIDTH), BF16),
        scratch_shapes=[pltpu.VMEM((tc + 2 * HALO, UNIT), F32)],
        compiler_params=_cparams(3),
        name="conv_silu",
    )(z, z, z, conv_w, conv_b)


def _log_sigmoid(x):
    return jnp.minimum(x, 0.0) - jnp.log(1.0 + jnp.exp(-jnp.abs(x)))


class _Stream:
    pass


def _mlstm_kernel(qf_ref, kf_ref, vf_ref, gcolf_ref, growf_ref,
                  qb_ref, kb_ref, vb_ref, gcolb_ref, growb_ref,
                  bcol_ref, brow_ref, hf_ref, hb_ref, c_ref, n_ref, m_ref):
    @pl.when(pl.program_id(1) == 0)
    def _():
        c_ref[...] = jnp.zeros_like(c_ref)
        n_ref[...] = jnp.zeros_like(n_ref)
        m_ref[...] = jnp.zeros_like(m_ref)

    L = ML_CHUNK
    dh = ML_HEAD_DIM
    t_idx = lax.broadcasted_iota(jnp.int32, (L, L), 0)
    s_idx = lax.broadcasted_iota(jnp.int32, (L, L), 1)
    per_direction = (
        (qf_ref, kf_ref, vf_ref, gcolf_ref, growf_ref, hf_ref, s_idx <= t_idx, t_idx <= s_idx),
        (qb_ref, kb_ref, vb_ref, gcolb_ref, growb_ref, hb_ref, s_idx >= t_idx, t_idx >= s_idx),
    )

    streams = []
    for direction, (q_ref, k_ref, v_ref, gcol_ref, grow_ref, h_ref, seen, seen_t) in enumerate(per_direction):
        seen_f = seen.astype(F32)
        seen_t_f = seen_t.astype(F32)
        g_col = gcol_ref[...] + bcol_ref[direction]
        g_row = grow_ref[...] + brow_ref[direction]
        for h in range(ML_HEADS):
            sl = slice(h * dh, (h + 1) * dh)
            s = _Stream()
            s.state, s.sl, s.h_ref, s.seen = direction * ML_HEADS + h, sl, h_ref, seen
            s.q, s.k, s.v = q_ref[:, sl], k_ref[:, sl], v_ref[:, sl]
            s.i_col = g_col[:, h:h + 1]
            s.i_row = g_row[h:h + 1, :]
            f_col = _log_sigmoid(g_col[:, ML_HEADS + h:ML_HEADS + h + 1])
            f_row = _log_sigmoid(g_row[ML_HEADS + h:ML_HEADS + h + 1, :])
            s.b_col = jnp.sum(seen_f * f_row, axis=1, keepdims=True)
            s.b_row = jnp.sum(seen_t_f * f_col, axis=0, keepdims=True)
            s.b_last = jnp.sum(f_row, axis=1, keepdims=True)
            streams.append(s)
    for s in streams:
        s.dmat = jnp.where(s.seen, s.b_col - s.b_row + s.i_row, NEG_BIG)
        s.dmax = jnp.max(s.dmat, axis=1, keepdims=True)
    for s in streams:
        s.m_prev = m_ref[s.state]
        s.c_prev = c_ref[s.state]
        s.n_prev = n_ref[s.state]
        s.qk = lax.dot_general(s.q, s.k, (((1,), (1,)), ((), ())), preferred_element_type=F32)
        s.qc = jnp.dot(s.q, s.c_prev.astype(BF16), preferred_element_type=F32)
    for s in streams:
        inter = s.b_col + s.m_prev
        s.mt = jnp.maximum(inter, s.dmax)
        s.sc = s.qk * jnp.exp(s.dmat - s.mt)
        s.a_inter = jnp.exp(inter - s.mt)
    for s in streams:
        s.num = jnp.dot(s.sc.astype(BF16), s.v, preferred_element_type=F32) + s.a_inter * s.qc
        s.den = (jnp.sum(s.sc, axis=1, keepdims=True)
                 + s.a_inter * jnp.sum(s.q.astype(F32) * s.n_prev, axis=1, keepdims=True))
    for s in streams:
        s.h_ref[:, s.sl] = (s.num / jnp.maximum(jnp.abs(s.den), jnp.exp(-s.mt))).astype(s.h_ref.dtype)
    for s in streams:
        g_end = s.b_last - s.b_col + s.i_col
        s.m_new = jnp.maximum(s.b_last + s.m_prev, jnp.max(g_end, axis=0, keepdims=True))
        s.wk = s.k.astype(F32) * jnp.exp(g_end - s.m_new)
        s.decay = jnp.exp(s.b_last + s.m_prev - s.m_new)
    for s in streams:
        c_ref[s.state] = s.decay * s.c_prev + lax.dot_general(
            s.wk.astype(BF16), s.v, (((0,), (0,)), ((), ())), preferred_element_type=F32)
        n_ref[s.state] = s.decay * s.n_prev + jnp.sum(s.wk, axis=0, keepdims=True)
        m_ref[s.state] = s.m_new


def _mlstm(qk, z, gates_col, gates_row, bias_col, bias_row, *, batch, seq):
    assert seq % ML_CHUNK == 0
    nc = seq // ML_CHUNK
    L = ML_CHUNK
    n_state = 2 * ML_HEADS
    fwd = lambda b, c: b * nc + c
    bwd = lambda b, c: b * nc + nc - 1 - c

    def chunk_specs(direction, tok):
        return [
            pl.BlockSpec((L, ML_WIDTH), lambda b, c: (tok(b, c), 0)),
            pl.BlockSpec((L, ML_WIDTH), lambda b, c: (tok(b, c), 1)),
            pl.BlockSpec((L, UNIT), lambda b, c: (tok(b, c), U_V_ML)),
            pl.BlockSpec((None, L, 2 * ML_HEADS), lambda b, c: (direction, tok(b, c), 0)),
            pl.BlockSpec((None, 2 * ML_HEADS, L), lambda b, c: (direction, 0, tok(b, c))),
        ]

    out_sds = jax.ShapeDtypeStruct((batch * seq, ML_WIDTH), BF16)
    return pl.pallas_call(
        _mlstm_kernel,
        grid=(batch, nc),
        in_specs=chunk_specs(0, fwd) + chunk_specs(1, bwd) + [
            pl.BlockSpec((2, 1, 2 * ML_HEADS), lambda b, c: (0, 0, 0)),
            pl.BlockSpec((2, 2 * ML_HEADS, 1), lambda b, c: (0, 0, 0)),
        ],
        out_specs=[
            pl.BlockSpec((L, ML_WIDTH), lambda b, c: (fwd(b, c), 0)),
            pl.BlockSpec((L, ML_WIDTH), lambda b, c: (bwd(b, c), 0)),
        ],
        out_shape=[out_sds, out_sds],
        scratch_shapes=[
            pltpu.VMEM((n_state, ML_HEAD_DIM, ML_HEAD_DIM), F32),
            pltpu.VMEM((n_state, 1, ML_HEAD_DIM), F32),
            pltpu.VMEM((n_state, 1, 1), F32),
        ],
        compiler_params=_cparams(2),
        name="mlstm",
    )(qk, qk, z, gates_col, gates_row, qk, qk, z, gates_col, gates_row, bias_col, bias_row)


def _merge_ln_kernel(x_ref, hf_ref, hb_ref, o_ref, ya_ref, ga_ref, gm_ref, ng_ref,
                     wpa_ref, wpm_ref, wout_ref, g_ref, b_ref, out_ref):
    h = hf_ref[...].astype(F32) + hb_ref[...].astype(F32)
    parts = []
    for hh in range(ML_HEADS):
        sl = slice(hh * ML_HEAD_DIM, (hh + 1) * ML_HEAD_DIM)
        hs = h[:, sl]
        mu = jnp.mean(hs, axis=-1, keepdims=True)
        hc = hs - mu
        var = jnp.mean(hc * hc, axis=-1, keepdims=True)
        parts.append(hc * lax.rsqrt(var + LN_EPS))
    y_m = jnp.concatenate(parts, axis=-1) * ng_ref[...] * _sigmoid(o_ref[...].astype(F32))
    pa = jnp.dot(ya_ref[...], wpa_ref[...], preferred_element_type=F32)
    pm = jnp.dot(y_m.astype(BF16), wpm_ref[...], preferred_element_type=F32)
    merged = _sigmoid(ga_ref[...].astype(F32)) * pa + _sigmoid(gm_ref[...].astype(F32)) * pm
    mix = jnp.dot(merged.astype(BF16), wout_ref[...], preferred_element_type=F32)
    out_ref[...] = _layernorm_rows(ALPHA * x_ref[...] + mix, g_ref[...], b_ref[...])


def _merge_ln(x, h_fwd, h_bwd, z, y_a, norm_g, w_pa, w_pm, w_out, g, b, *, tm=256):
    t, d = x.shape
    assert t % tm == 0 and U_G_NA % 2 == 0 and U_G_ML % 2 == 0
    wide = 2 * UNIT
    const = lambda i: (0, 0)
    return pl.pallas_call(
        _merge_ln_kernel,
        grid=(t // tm,),
        in_specs=[
            pl.BlockSpec((tm, d), lambda i: (i, 0)),
            pl.BlockSpec((tm, ML_WIDTH), lambda i: (i, 0)),
            pl.BlockSpec((tm, ML_WIDTH), lambda i: (i, 0)),
            pl.BlockSpec((tm, UNIT), lambda i: (i, U_O_ML)),
            pl.BlockSpec((tm, NA_WIDTH), lambda i: (i, 0)),
            pl.BlockSpec((tm, wide), lambda i: (i, U_G_NA // 2)),
            pl.BlockSpec((tm, wide), lambda i: (i, U_G_ML // 2)),
            pl.BlockSpec((1, ML_WIDTH), const),
            pl.BlockSpec((NA_WIDTH, d), const, pipeline_mode=pl.Buffered(1)),
            pl.BlockSpec((ML_WIDTH, d), const, pipeline_mode=pl.Buffered(1)),
            pl.BlockSpec((d, d), const, pipeline_mode=pl.Buffered(1)),
            pl.BlockSpec((1, d), const),
            pl.BlockSpec((1, d), const),
        ],
        out_specs=pl.BlockSpec((tm, d), lambda i: (i, 0)),
        out_shape=jax.ShapeDtypeStruct((t, d), F32),
        compiler_params=_cparams(1),
        name="merge_ln",
    )(x, h_fwd, h_bwd, z, y_a, z, z, norm_g, w_pa, w_pm, w_out, g, b)


def _prepare_params(ffa_w_gu, ffa_w_down, norm_a_g, norm_a_b, mix_w_in, na_rpb, ml_conv_w, ml_conv_b,
                    ml_gate_b, ml_norm_g, mix_w_pa, mix_w_pm, mix_w_out, norm_m_g, norm_m_b,
                    ffb_w_gu, ffb_w_down, norm_b_g, norm_b_b, l):
    sizes = (NA_WIDTH, NA_WIDTH, NA_WIDTH, 2 * ML_WIDTH, ML_WIDTH, ML_WIDTH, N_GATE, D_MODEL, D_MODEL)
    points = np.cumsum(sizes)[:-1].tolist()
    w_qa, w_ka, w_va, w_qkm, w_vm, w_om, w_gates, w_ga, w_gm = jnp.split(
        mix_w_in[l].astype(BF16), points, axis=1)
    w_in = jnp.concatenate([w_ga, w_gm, w_qa, w_ka, w_va, w_qkm, w_vm, w_om], axis=1)
    assert w_in.shape[1] == N_UNITS * UNIT
    w_gate = jnp.pad(w_gates, ((0, 0), (0, GATE_PAD - N_GATE)))
    gb = ml_gate_b[l].astype(F32).reshape(2, 2, ML_HEADS).transpose(1, 0, 2).reshape(2, 2 * ML_HEADS)
    row = lambda v: v.reshape(1, -1).astype(F32)
    return dict(
        ffa_w_gu=_block_gu(ffa_w_gu[l]), ffa_w_down=ffa_w_down[l].astype(BF16),
        norm_a_g=row(norm_a_g[l]), norm_a_b=row(norm_a_b[l]),
        w_in=w_in, w_gate=w_gate,
        na_tables=_na_bias_tables(na_rpb[l]),
        conv_w=ml_conv_w[l].astype(F32), conv_b=row(ml_conv_b[l]),
        gate_b_col=gb.reshape(2, 1, 2 * ML_HEADS), gate_b_row=gb.reshape(2, 2 * ML_HEADS, 1),
        ml_norm_g=row(ml_norm_g[l]),
        w_pa=mix_w_pa[l].astype(BF16), w_pm=mix_w_pm[l].astype(BF16), w_out=mix_w_out[l].astype(BF16),
        norm_m_g=row(norm_m_g[l]), norm_m_b=row(norm_m_b[l]),
        ffb_w_gu=_block_gu(ffb_w_gu[l]), ffb_w_down=ffb_w_down[l].astype(BF16),
        norm_b_g=row(norm_b_g[l]), norm_b_b=row(norm_b_b[l]),
    )


def _encoder_layer(x, p):
    batch, seq, d = x.shape
    t = batch * seq
    x0 = x.reshape(t, d)
    x1 = _ffn_ln(x0, p["ffa_w_gu"], p["ffa_w_down"], p["norm_a_g"], p["norm_a_b"])
    z, gates = _in_proj(x1, p["w_in"], p["w_gate"])
    y_a = _neighbourhood_attention(z, p["na_tables"], batch=batch, seq=seq)
    qk = _conv_silu(z, p["conv_w"], p["conv_b"], batch=batch, seq=seq)
    gd = gates[:, :N_GATE].reshape(t, 2, 2, ML_HEADS)
    gates_col = gd.transpose(2, 0, 1, 3).reshape(2, t, 2 * ML_HEADS)
    gates_row = gd.transpose(2, 1, 3, 0).reshape(2, 2 * ML_HEADS, t)
    h_fwd, h_bwd = _mlstm(qk, z, gates_col, gates_row, p["gate_b_col"], p["gate_b_row"],
                          batch=batch, seq=seq)
    x2 = _merge_ln(x1, h_fwd, h_bwd, z, y_a, p["ml_norm_g"], p["w_pa"], p["w_pm"], p["w_out"],
                   p["norm_m_g"], p["norm_m_b"])
    y = _ffn_ln(x2, p["ffb_w_gu"], p["ffb_w_down"], p["norm_b_g"], p["norm_b_b"])
    return y.reshape(batch, seq, d)


def kernel(x_prompt, x_sample, ffa_w_gu, ffa_w_down, norm_a_g, norm_a_b, mix_w_in, na_rpb, ml_conv_w, ml_conv_b, ml_gate_b, ml_norm_g, mix_w_pa, mix_w_pm, mix_w_out, norm_m_g, norm_m_b, ffb_w_gu, ffb_w_down, norm_b_g, norm_b_b):
    assert ffa_w_gu.shape[0] == DEPTH
    y_prompt, y_sample = x_prompt, x_sample
    for l in range(DEPTH):
        p = _prepare_params(ffa_w_gu, ffa_w_down, norm_a_g, norm_a_b, mix_w_in, na_rpb, ml_conv_w,
                            ml_conv_b, ml_gate_b, ml_norm_g, mix_w_pa, mix_w_pm, mix_w_out, norm_m_g,
                            norm_m_b, ffb_w_gu, ffb_w_down, norm_b_g, norm_b_b, l)
        y_prompt = _encoder_layer(y_prompt, p)
        y_sample = _encoder_layer(y_sample, p)
    return (y_prompt, y_sample)
```

```python
import functools

import jax
import jax.numpy as jnp
import numpy as np
from jax import lax
from jax.experimental import pallas as pl
from jax.experimental.pallas import tpu as pltpu

D_MODEL = 2048
D_FF = 5632
DEPTH = 1
ALPHA = (2 * DEPTH) ** 0.25
LN_EPS = 1e-5

GRID_W = 64
NA_HEADS = 8
NA_HEAD_DIM = 128
NA_WIDTH = NA_HEADS * NA_HEAD_DIM
NA_KH = 8
NA_KW = 16
NA_QROWS = 4
NA_KROWS = 12
NA_BLOCKS_PER_STEP = 4
NA_RPB_ROWS = 2 * NA_KH - 1
NA_RPB_COLS = 2 * NA_KW - 1

ML_HEADS = 4
ML_HEAD_DIM = 256
ML_WIDTH = ML_HEADS * ML_HEAD_DIM
ML_CHUNK = 128
ML_CONV = 5
N_GATE = 4 * ML_HEADS

UNIT = 1024
U_G_NA, U_G_ML, U_Q_NA, U_K_NA, U_V_NA, U_Q_ML, U_K_ML, U_V_ML, U_O_ML = 0, 2, 4, 5, 6, 7, 8, 9, 10
N_UNITS = 11
GATE_PAD = 128

NEG_BIG = -1e30
VMEM_LIMIT = 60 * 1024 * 1024

F32 = jnp.float32
BF16 = jnp.bfloat16


def _cparams(n_grid_dims):
    return pltpu.CompilerParams(dimension_semantics=("arbitrary",) * n_grid_dims,
                                vmem_limit_bytes=VMEM_LIMIT)


def _sigmoid(x):
    return 1.0 / (1.0 + jnp.exp(-x))


def _layernorm_rows(y, g, b):
    mu = jnp.mean(y, axis=-1, keepdims=True)
    yc = y - mu
    var = jnp.mean(yc * yc, axis=-1, keepdims=True)
    return yc * lax.rsqrt(var + LN_EPS) * g + b


def _ffn_ln_kernel(x_ref, wg_ref, wu_ref, wd_ref, g_ref, b_ref, o_ref, xb_ref):
    j = pl.program_id(1)
    last = pl.num_programs(1) - 1
    tm = o_ref.shape[0]

    def partial_sum(rows):
        xb = xb_ref[rows, :]
        gate = jnp.dot(xb, wg_ref[...], preferred_element_type=F32)
        up = jnp.dot(xb, wu_ref[...], preferred_element_type=F32)
        act = gate * _sigmoid(gate) * up
        return jnp.dot(act.astype(BF16), wd_ref[...], preferred_element_type=F32)

    def chunks(n_rows):
        return [pl.ds(r * n_rows, n_rows) for r in range(tm // n_rows)]

    @pl.when(j == 0)
    def _():
        for rows in chunks(FFN_FIRST_ROWS):
            xb_ref[rows, :] = x_ref[rows, :].astype(BF16)
        for rows in chunks(FFN_FIRST_ROWS):
            o_ref[rows, :] = partial_sum(rows)

    @pl.when((j > 0) & (j < last))
    def _():
        o_ref[...] += partial_sum(slice(None))

    @pl.when(j == last)
    def _():
        def finish(rows, acc):
            y = ALPHA * x_ref[rows, :] + 0.5 * acc
            o_ref[rows, :] = _layernorm_rows(y, g_ref[...], b_ref[...])

        pending = None
        for rows in chunks(FFN_LAST_ROWS):
            acc = o_ref[rows, :] + partial_sum(rows)
            if pending is not None:
                finish(*pending)
            pending = (rows, acc)
        finish(*pending)


FFN_FIRST_ROWS = 512
FFN_LAST_ROWS = 256
FFN_TM = 1024
FFN_TF = 256


def _ffn_ln(x, w_gu, w_down, g, b):
    t, d = x.shape
    d_ff = w_down.shape[0]
    tm, tf = FFN_TM, FFN_TF
    n_ff = d_ff // tf
    assert t % tm == 0 and d_ff % tf == 0 and w_gu.shape == (d, 2 * d_ff) and n_ff >= 2
    return pl.pallas_call(
        _ffn_ln_kernel,
        grid=(t // tm, n_ff),
        in_specs=[
            pl.BlockSpec((tm, d), lambda i, j: (i, 0)),
            pl.BlockSpec((d, tf), lambda i, j: (0, j)),
            pl.BlockSpec((d, tf), lambda i, j: (0, j + n_ff)),
            pl.BlockSpec((tf, d), lambda i, j: (j, 0)),
            pl.BlockSpec((1, d), lambda i, j: (0, 0)),
            pl.BlockSpec((1, d), lambda i, j: (0, 0)),
        ],
        out_specs=pl.BlockSpec((tm, d), lambda i, j: (i, 0)),
        out_shape=jax.ShapeDtypeStruct((t, d), F32),
        scratch_shapes=[pltpu.VMEM((tm, d), BF16)],
        compiler_params=_cparams(2),
        name="ffn_ln",
    )(x, w_gu, w_gu, w_down, g, b)


PROJ_TM = 1024


def _in_proj_kernel(x_ref, w_ref, wgate_ref, z_ref, gates_ref, xb_ref):
    j = pl.program_id(1)

    @pl.when(j == 0)
    def _():
        xb_ref[...] = x_ref[...].astype(BF16)
        gates_ref[...] = jnp.dot(xb_ref[...], wgate_ref[...], preferred_element_type=F32)

    scale = jnp.where(j == U_Q_NA, NA_HEAD_DIM ** -0.5, 1.0)
    z = jnp.dot(xb_ref[...], w_ref[...], preferred_element_type=F32)
    z_ref[...] = (z * scale).astype(z_ref.dtype)


def _in_proj(x, w, w_gate):
    t, d = x.shape
    n_units = w.shape[1] // UNIT
    tm = PROJ_TM
    assert t % tm == 0 and w.shape[1] % UNIT == 0
    return pl.pallas_call(
        _in_proj_kernel,
        grid=(t // tm, n_units),
        in_specs=[
            pl.BlockSpec((tm, d), lambda i, j: (i, 0)),
            pl.BlockSpec((d, UNIT), lambda i, j: (0, j)),
            pl.BlockSpec((d, GATE_PAD), lambda i, j: (0, 0)),
        ],
        out_specs=[
            pl.BlockSpec((tm, UNIT), lambda i, j: (i, j)),
            pl.BlockSpec((tm, GATE_PAD), lambda i, j: (i, 0)),
        ],
        out_shape=[jax.ShapeDtypeStruct((t, n_units * UNIT), BF16),
                   jax.ShapeDtypeStruct((t, GATE_PAD), F32)],
        scratch_shapes=[pltpu.VMEM((tm, d), BF16)],
        compiler_params=_cparams(2),
        name="in_proj",
    )(x, w, w_gate)


def _na_bias_tables(rpb):
    heads = rpb.shape[0]
    c = np.arange(GRID_W)[:, None]
    kc = np.arange(GRID_W)[None, :]
    ws = np.clip(c - NA_KW // 2, 0, GRID_W - NA_KW)
    col_ok = (kc >= ws) & (kc < ws + NA_KW)
    period = 2 * GRID_W
    padded = jnp.pad(rpb.astype(F32), ((0, 0), (0, 0), (0, period - NA_RPB_COLS)))
    flat = jnp.broadcast_to(padded[:, :, None, :], (heads, NA_RPB_ROWS, GRID_W, period))
    flat = flat.reshape(heads, NA_RPB_ROWS, GRID_W * period)
    skew = flat[:, :, NA_KW - 1:NA_KW - 1 + GRID_W * (period - 1)]
    colb = skew.reshape(heads, NA_RPB_ROWS, GRID_W, period - 1)[..., :GRID_W]
    colb = jnp.where(jnp.asarray(col_ok)[None, None], colb, NEG_BIG)
    colb = colb.transpose(0, 2, 1, 3)

    def masked(n_rows):
        return jnp.full((heads, GRID_W, n_rows, GRID_W), NEG_BIG, F32)

    tables = []
    for ty in range(3):
        per_query_row = []
        for a in range(NA_QROWS):
            if ty == 0:
                start, q_row = 0, a
            elif ty == 1:
                start, q_row = a, a + NA_KH // 2
            else:
                start, q_row = NA_KROWS - NA_KH, a + NA_KROWS - NA_QROWS
            first_rpb_row = start - q_row + NA_KH - 1
            assert 0 <= first_rpb_row and first_rpb_row + NA_KH <= NA_RPB_ROWS
            per_query_row.append(jnp.concatenate(
                [masked(start), colb[:, :, first_rpb_row:first_rpb_row + NA_KH],
                 masked(NA_KROWS - NA_KH - start)], axis=2))
        tab = jnp.stack(per_query_row, axis=1)
        tables.append(tab.reshape(heads, NA_QROWS * GRID_W, NA_KROWS * GRID_W))
    return jnp.stack(tables)


def _na_kernel(q_ref, k_ref, v_ref, tab_first_ref, tab_mid_ref, tab_last_ref, o_ref, *, rows):
    i = pl.program_id(2)
    nq = NA_QROWS * GRID_W
    nk = NA_KROWS * GRID_W
    tab_refs = (tab_first_ref,) + (tab_mid_ref,) * (NA_BLOCKS_PER_STEP - 2) + (tab_last_ref,)
    starts, scores = [], []
    for blk in range(NA_BLOCKS_PER_STEP):
        q_row0 = NA_QROWS * (NA_BLOCKS_PER_STEP * i + blk)
        first_key_row = jnp.clip(q_row0 - NA_KH // 2, 0, rows - NA_KROWS)
        start = pl.multiple_of(first_key_row * GRID_W, GRID_W * NA_QROWS)
        q = q_ref[blk * nq:(blk + 1) * nq, :]
        k = k_ref[pl.ds(start, nk), :]
        s = lax.dot_general(q, k, (((1,), (1,)), ((), ())), preferred_element_type=F32)
        starts.append(start)
        scores.append(s + tab_refs[blk][...])
    probs, sums = [], []
    for s in scores:
        m = jnp.max(s, axis=-1, keepdims=True)
        p = jnp.exp(s - m)
        sums.append(jnp.sum(p, axis=-1, keepdims=True))
        probs.append(p.astype(BF16))
    for blk in range(NA_BLOCKS_PER_STEP):
        v = v_ref[pl.ds(starts[blk], nk), :]
        o = jnp.dot(probs[blk], v, preferred_element_type=F32)
        o_ref[blk * nq:(blk + 1) * nq, :] = (o / sums[blk]).astype(o_ref.dtype)


def _neighbourhood_attention(z, tables, *, batch, seq):
    rows = seq // GRID_W
    step_rows = NA_QROWS * NA_BLOCKS_PER_STEP
    assert NA_BLOCKS_PER_STEP >= 2 and NA_KROWS == NA_KH + NA_QROWS
    assert seq % GRID_W == 0 and rows % step_rows == 0 and rows >= 2 * NA_KROWS
    nstep = rows // step_rows
    nq_step = step_rows * GRID_W
    hd = NA_HEAD_DIM
    per_unit = UNIT // hd
    tab_shape = (None, None, NA_QROWS * GRID_W, NA_KROWS * GRID_W)

    return pl.pallas_call(
        functools.partial(_na_kernel, rows=rows),
        grid=(batch, NA_HEADS, nstep),
        in_specs=[
            pl.BlockSpec((nq_step, hd), lambda b, h, i: (b * nstep + i, U_Q_NA * per_unit + h)),
            pl.BlockSpec((seq, hd), lambda b, h, i: (b, U_K_NA * per_unit + h)),
            pl.BlockSpec((seq, hd), lambda b, h, i: (b, U_V_NA * per_unit + h)),
            pl.BlockSpec(tab_shape, lambda b, h, i: (jnp.where(i == 0, 0, 1), h, 0, 0)),
            pl.BlockSpec(tab_shape, lambda b, h, i: (1, h, 0, 0)),
            pl.BlockSpec(tab_shape, lambda b, h, i: (jnp.where(i == nstep - 1, 2, 1), h, 0, 0)),
        ],
        out_specs=pl.BlockSpec((nq_step, hd), lambda b, h, i: (b * nstep + i, h)),
        out_shape=jax.ShapeDtypeStruct((batch * seq, NA_WIDTH), BF16),
        compiler_params=_cparams(3),
        name="na",
    )(z, z, z, tables, tables, tables)


HALO = 16


def _conv_silu_kernel(prev_ref, cur_ref, next_ref, w_ref, b_ref, o_ref, ext_ref, *, tc):
    t = pl.program_id(1)
    part = pl.program_id(2)
    prev_keep = jnp.where(t == 0, 0.0, 1.0)
    next_keep = jnp.where(t == pl.num_programs(1) - 1, 0.0, 1.0)
    ext_ref[0:HALO, :] = prev_ref[...].astype(F32) * prev_keep
    ext_ref[HALO:HALO + tc, :] = cur_ref[...].astype(F32)
    ext_ref[HALO + tc:, :] = next_ref[...].astype(F32) * next_keep
    pad = ML_CONV // 2
    ext = ext_ref[...]
    n_ext = ext.shape[0]
    y = jnp.zeros((tc, cur_ref.shape[1]), F32) + b_ref[...]
    for j in range(ML_CONV):
        rolled = ext if j == pad else pltpu.roll(ext, (pad - j) % n_ext, axis=0)
        y = y + rolled[HALO:HALO + tc, :] * w_ref[j:j + 1, :]
    y = y * _sigmoid(y)
    scale = jnp.where(part == 1, ML_HEAD_DIM ** -0.5, 1.0)
    o_ref[...] = (y * scale).astype(o_ref.dtype)


def _conv_silu(z, conv_w, conv_b, *, batch, seq, tc=512):
    assert seq % tc == 0 and tc % HALO == 0 and U_K_ML == U_Q_ML + 1
    nt = seq // tc
    per = tc // HALO
    last_halo = batch * seq // HALO - 1

    return pl.pallas_call(
        functools.partial(_conv_silu_kernel, tc=tc),
        grid=(batch, nt, 2),
        in_specs=[
            pl.BlockSpec((HALO, UNIT), lambda b, t, p: (jnp.maximum((b * nt + t) * per - 1, 0), U_Q_ML + p)),
            pl.BlockSpec((tc, UNIT), lambda b, t, p: (b * nt + t, U_Q_ML + p)),
            pl.BlockSpec((HALO, UNIT), lambda b, t, p: (jnp.minimum((b * nt + t + 1) * per, last_halo), U_Q_ML + p)),
            pl.BlockSpec((ML_CONV, UNIT), lambda b, t, p: (0, p)),
            pl.BlockSpec((1, UNIT), lambda b, t, p: (0, p)),
        ],
        out_specs=pl.BlockSpec((tc, UNIT), lambda b, t, p: (b * nt + t, p)),
        out_shape=jax.ShapeDtypeStruct((batch * seq, 2 * ML_WIDTH), BF16),
        scratch_shapes=[pltpu.VMEM((tc + 2 * HALO, UNIT), F32)],
        compiler_params=_cparams(3),
        name="conv_silu",
    )(z, z, z, conv_w, conv_b)


def _log_sigmoid(x):
    return jnp.minimum(x, 0.0) - jnp.log(1.0 + jnp.exp(-jnp.abs(x)))


class _Stream:
    pass


def _mlstm_kernel(qf_ref, kf_ref, vf_ref, gcolf_ref, growf_ref,
                  qb_ref, kb_ref, vb_ref, gcolb_ref, growb_ref,
                  bcol_ref, brow_ref, hf_ref, hb_ref, c_ref, n_ref, m_ref):
    @pl.when(pl.program_id(1) == 0)
    def _():
        c_ref[...] = jnp.zeros_like(c_ref)
        n_ref[...] = jnp.zeros_like(n_ref)
        m_ref[...] = jnp.zeros_like(m_ref)

    L = ML_CHUNK
    dh = ML_HEAD_DIM
    t_idx = lax.broadcasted_iota(jnp.int32, (L, L), 0)
    s_idx = lax.broadcasted_iota(jnp.int32, (L, L), 1)
    per_direction = (
        (qf_ref, kf_ref, vf_ref, gcolf_ref, growf_ref, hf_ref, s_idx <= t_idx, t_idx <= s_idx),
        (qb_ref, kb_ref, vb_ref, gcolb_ref, growb_ref, hb_ref, s_idx >= t_idx, t_idx >= s_idx),
    )

    streams = []
    for direction, (q_ref, k_ref, v_ref, gcol_ref, grow_ref, h_ref, seen, seen_t) in enumerate(per_direction):
        seen_f = seen.astype(F32)
        seen_t_f = seen_t.astype(F32)
        g_col = gcol_ref[...] + bcol_ref[direction]
        g_row = grow_ref[...] + brow_ref[direction]
        for h in range(ML_HEADS):
            sl = slice(h * dh, (h + 1) * dh)
            s = _Stream()
            s.state, s.sl, s.h_ref, s.seen = direction * ML_HEADS + h, sl, h_ref, seen
            s.q, s.k, s.v = q_ref[:, sl], k_ref[:, sl], v_ref[:, sl]
            s.i_col = g_col[:, h:h + 1]
            s.i_row = g_row[h:h + 1, :]
            f_col = _log_sigmoid(g_col[:, ML_HEADS + h:ML_HEADS + h + 1])
            f_row = _log_sigmoid(g_row[ML_HEADS + h:ML_HEADS + h + 1, :])
            s.b_col = jnp.sum(seen_f * f_row, axis=1, keepdims=True)
            s.b_row = jnp.sum(seen_t_f * f_col, axis=0, keepdims=True)
            s.b_last = jnp.sum(f_row, axis=1, keepdims=True)
            streams.append(s)
    for s in streams:
        s.dmat = jnp.where(s.seen, s.b_col - s.b_row + s.i_row, NEG_BIG)
        s.dmax = jnp.max(s.dmat, axis=1, keepdims=True)
    for s in streams:
        s.m_prev = m_ref[s.state]
        s.c_prev = c_ref[s.state]
        s.n_prev = n_ref[s.state]
        s.qk = lax.dot_general(s.q, s.k, (((1,), (1,)), ((), ())), preferred_element_type=F32)
        s.qc = jnp.dot(s.q, s.c_prev.astype(BF16), preferred_element_type=F32)
    for s in streams:
        inter = s.b_col + s.m_prev
        s.mt = jnp.maximum(inter, s.dmax)
        s.sc = s.qk * jnp.exp(s.dmat - s.mt)
        s.a_inter = jnp.exp(inter - s.mt)
    for s in streams:
        s.num = jnp.dot(s.sc.astype(BF16), s.v, preferred_element_type=F32) + s.a_inter * s.qc
        s.den = (jnp.sum(s.sc, axis=1, keepdims=True)
                 + s.a_inter * jnp.sum(s.q.astype(F32) * s.n_prev, axis=1, keepdims=True))
    for s in streams:
        s.h_ref[:, s.sl] = (s.num / jnp.maximum(jnp.abs(s.den), jnp.exp(-s.mt))).astype(s.h_ref.dtype)
    for s in streams:
        g_end = s.b_last - s.b_col + s.i_col
        s.m_new = jnp.maximum(s.b_last + s.m_prev, jnp.max(g_end, axis=0, keepdims=True))
        s.wk = s.k.astype(F32) * jnp.exp(g_end - s.m_new)
        s.decay = jnp.exp(s.b_last + s.m_prev - s.m_new)
    for s in streams:
        c_ref[s.state] = s.decay * s.c_prev + lax.dot_general(
            s.wk.astype(BF16), s.v, (((0,), (0,)), ((), ())), preferred_element_type=F32)
        n_ref[s.state] = s.decay * s.n_prev + jnp.sum(s.wk, axis=0, keepdims=True)
        m_ref[s.state] = s.m_new


def _mlstm(qk, z, gates_col, gates_row, bias_col, bias_row, *, batch, seq):
    assert seq % ML_CHUNK == 0
    nc = seq // ML_CHUNK
    L = ML_CHUNK
    n_state = 2 * ML_HEADS
    fwd = lambda b, c: b * nc + c
    bwd = lambda b, c: b * nc + nc - 1 - c

    def chunk_specs(direction, tok):
        return [
            pl.BlockSpec((L, ML_WIDTH), lambda b, c: (tok(b, c), 0)),
            pl.BlockSpec((L, ML_WIDTH), lambda b, c: (tok(b, c), 1)),
            pl.BlockSpec((L, UNIT), lambda b, c: (tok(b, c), U_V_ML)),
            pl.BlockSpec((None, L, 2 * ML_HEADS), lambda b, c: (direction, tok(b, c), 0)),
            pl.BlockSpec((None, 2 * ML_HEADS, L), lambda b, c: (direction, 0, tok(b, c))),
        ]

    out_sds = jax.ShapeDtypeStruct((batch * seq, ML_WIDTH), BF16)
    return pl.pallas_call(
        _mlstm_kernel,
        grid=(batch, nc),
        in_specs=chunk_specs(0, fwd) + chunk_specs(1, bwd) + [
            pl.BlockSpec((2, 1, 2 * ML_HEADS), lambda b, c: (0, 0, 0)),
            pl.BlockSpec((2, 2 * ML_HEADS, 1), lambda b, c: (0, 0, 0)),
        ],
        out_specs=[
            pl.BlockSpec((L, ML_WIDTH), lambda b, c: (fwd(b, c), 0)),
            pl.BlockSpec((L, ML_WIDTH), lambda b, c: (bwd(b, c), 0)),
        ],
        out_shape=[out_sds, out_sds],
        scratch_shapes=[
            pltpu.VMEM((n_state, ML_HEAD_DIM, ML_HEAD_DIM), F32),
            pltpu.VMEM((n_state, 1, ML_HEAD_DIM), F32),
            pltpu.VMEM((n_state, 1, 1), F32),
        ],
        compiler_params=_cparams(2),
        name="mlstm",
    )(qk, qk, z, gates_col, gates_row, qk, qk, z, gates_col, gates_row, bias_col, bias_row)


def _merge_ln_kernel(x_ref, hf_ref, hb_ref, o_ref, ya_ref, ga_ref, gm_ref, ng_ref,
                     wpa_ref, wpm_ref, wout_ref, g_ref, b_ref, out_ref):
    h = hf_ref[...].astype(F32) + hb_ref[...].astype(F32)
    parts = []
    for hh in range(ML_HEADS):
        sl = slice(hh * ML_HEAD_DIM, (hh + 1) * ML_HEAD_DIM)
        hs = h[:, sl]
        mu = jnp.mean(hs, axis=-1, keepdims=True)
        hc = hs - mu
        var = jnp.mean(hc * hc, axis=-1, keepdims=True)
        parts.append(hc * lax.rsqrt(var + LN_EPS))
    y_m = jnp.concatenate(parts, axis=-1) * ng_ref[...] * _sigmoid(o_ref[...].astype(F32))
    pa = jnp.dot(ya_ref[...], wpa_ref[...], preferred_element_type=F32)
    pm = jnp.dot(y_m.astype(BF16), wpm_ref[...], preferred_element_type=F32)
    merged = _sigmoid(ga_ref[...].astype(F32)) * pa + _sigmoid(gm_ref[...].astype(F32)) * pm
    mix = jnp.dot(merged.astype(BF16), wout_ref[...], preferred_element_type=F32)
    out_ref[...] = _layernorm_rows(ALPHA * x_ref[...] + mix, g_ref[...], b_ref[...])


def _merge_ln(x, h_fwd, h_bwd, z, y_a, norm_g, w_pa, w_pm, w_out, g, b, *, tm=256):
    t, d = x.shape
    assert t % tm == 0 and U_G_NA % 2 == 0 and U_G_ML % 2 == 0
    wide = 2 * UNIT
    const = lambda i: (0, 0)
    return pl.pallas_call(
        _merge_ln_kernel,
        grid=(t // tm,),
        in_specs=[
            pl.BlockSpec((tm, d), lambda i: (i, 0)),
            pl.BlockSpec((tm, ML_WIDTH), lambda i: (i, 0)),
            pl.BlockSpec((tm, ML_WIDTH), lambda i: (i, 0)),
            pl.BlockSpec((tm, UNIT), lambda i: (i, U_O_ML)),
            pl.BlockSpec((tm, NA_WIDTH), lambda i: (i, 0)),
            pl.BlockSpec((tm, wide), lambda i: (i, U_G_NA // 2)),
            pl.BlockSpec((tm, wide), lambda i: (i, U_G_ML // 2)),
            pl.BlockSpec((1, ML_WIDTH), const),
            pl.BlockSpec((NA_WIDTH, d), const, pipeline_mode=pl.Buffered(1)),
            pl.BlockSpec((ML_WIDTH, d), const, pipeline_mode=pl.Buffered(1)),
            pl.BlockSpec((d, d), const, pipeline_mode=pl.Buffered(1)),
            pl.BlockSpec((1, d), const),
            pl.BlockSpec((1, d), const),
        ],
        out_specs=pl.BlockSpec((tm, d), lambda i: (i, 0)),
        out_shape=jax.ShapeDtypeStruct((t, d), F32),
        compiler_params=_cparams(1),
        name="merge_ln",
    )(x, h_fwd, h_bwd, z, y_a, z, z, norm_g, w_pa, w_pm, w_out, g, b)


def _prepare_params(ffa_w_gu, ffa_w_down, norm_a_g, norm_a_b, mix_w_in, na_rpb, ml_conv_w, ml_conv_b,
                    ml_gate_b, ml_norm_g, mix_w_pa, mix_w_pm, mix_w_out, norm_m_g, norm_m_b,
                    ffb_w_gu, ffb_w_down, norm_b_g, norm_b_b, l):
    sizes = (NA_WIDTH, NA_WIDTH, NA_WIDTH, 2 * ML_WIDTH, ML_WIDTH, ML_WIDTH, N_GATE, D_MODEL, D_MODEL)
    points = np.cumsum(sizes)[:-1].tolist()
    w_qa, w_ka, w_va, w_qkm, w_vm, w_om, w_gates, w_ga, w_gm = jnp.split(
        mix_w_in[l].astype(BF16), points, axis=1)
    w_in = jnp.concatenate([w_ga, w_gm, w_qa, w_ka, w_va, w_qkm, w_vm, w_om], axis=1)
    assert w_in.shape[1] == N_UNITS * UNIT
    w_gate = jnp.pad(w_gates, ((0, 0), (0, GATE_PAD - N_GATE)))
    gb = ml_gate_b[l].astype(F32).reshape(2, 2, ML_HEADS).transpose(1, 0, 2).reshape(2, 2 * ML_HEADS)
    row = lambda v: v.reshape(1, -1).astype(F32)
    return dict(
        ffa_w_gu=ffa_w_gu[l].astype(BF16), ffa_w_down=ffa_w_down[l].astype(BF16),
        norm_a_g=row(norm_a_g[l]), norm_a_b=row(norm_a_b[l]),
        w_in=w_in, w_gate=w_gate,
        na_tables=_na_bias_tables(na_rpb[l]),
        conv_w=ml_conv_w[l].astype(F32), conv_b=row(ml_conv_b[l]),
        gate_b_col=gb.reshape(2, 1, 2 * ML_HEADS), gate_b_row=gb.reshape(2, 2 * ML_HEADS, 1),
        ml_norm_g=row(ml_norm_g[l]),
        w_pa=mix_w_pa[l].astype(BF16), w_pm=mix_w_pm[l].astype(BF16), w_out=mix_w_out[l].astype(BF16),
        norm_m_g=row(norm_m_g[l]), norm_m_b=row(norm_m_b[l]),
        ffb_w_gu=ffb_w_gu[l].astype(BF16), ffb_w_down=ffb_w_down[l].astype(BF16),
        norm_b_g=row(norm_b_g[l]), norm_b_b=row(norm_b_b[l]),
    )


def _encoder_layer(x, p):
    batch, seq, d = x.shape
    t = batch * seq
    x0 = x.reshape(t, d)
    x1 = _ffn_ln(x0, p["ffa_w_gu"], p["ffa_w_down"], p["norm_a_g"], p["norm_a_b"])
    z, gates = _in_proj(x1, p["w_in"], p["w_gate"])
    y_a = _neighbourhood_attention(z, p["na_tables"], batch=batch, seq=seq)
    qk = _conv_silu(z, p["conv_w"], p["conv_b"], batch=batch, seq=seq)
    gd = gates[:, :N_GATE].reshape(t, 2, 2, ML_HEADS)
    gates_col = gd.transpose(2, 0, 1, 3).reshape(2, t, 2 * ML_HEADS)
    gates_row = gd.transpose(2, 1, 3, 0).reshape(2, 2 * ML_HEADS, t)
    h_fwd, h_bwd = _mlstm(qk, z, gates_col, gates_row, p["gate_b_col"], p["gate_b_row"],
                          batch=batch, seq=seq)
    x2 = _merge_ln(x1, h_fwd, h_bwd, z, y_a, p["ml_norm_g"], p["w_pa"], p["w_pm"], p["w_out"],
                   p["norm_m_g"], p["norm_m_b"])
    y = _ffn_ln(x2, p["ffb_w_gu"], p["ffb_w_down"], p["norm_b_g"], p["norm_b_b"])
    return y.reshape(batch, seq, d)


def kernel(x_prompt, x_sample, ffa_w_gu, ffa_w_down, norm_a_g, norm_a_b, mix_w_in, na_rpb, ml_conv_w, ml_conv_b, ml_gate_b, ml_norm_g, mix_w_pa, mix_w_pm, mix_w_out, norm_m_g, norm_m_b, ffb_w_gu, ffb_w_down, norm_b_g, norm_b_b):
    assert ffa_w_gu.shape[0] == DEPTH
    y_prompt, y_sample = x_prompt, x_sample
    for l in range(DEPTH):
        p = _prepare_params(ffa_w_gu, ffa_w_down, norm_a_g, norm_a_b, mix_w_in, na_rpb, ml_conv_w,
                            ml_conv_b, ml_gate_b, ml_norm_g, mix_w_pa, mix_w_pm, mix_w_out, norm_m_g,
                            norm_m_b, ffb_w_gu, ffb_w_down, norm_b_g, norm_b_b, l)
        y_prompt = _encoder_layer(y_prompt, p)
        y_sample = _encoder_layer(y_sample, p)
    return (y_prompt, y_sample)
```

```python
import functools

import jax
import jax.numpy as jnp
import numpy as np
from jax import lax
from jax.experimental import pallas as pl
from jax.experimental.pallas import tpu as pltpu

D_MODEL = 2048
D_FF = 5632
DEPTH = 1
ALPHA = (2 * DEPTH) ** 0.25
LN_EPS = 1e-5

GRID_W = 64
NA_HEADS = 8
NA_HEAD_DIM = 128
NA_WIDTH = NA_HEADS * NA_HEAD_DIM
NA_KH = 8
NA_KW = 16
NA_QROWS = 4
NA_KROWS = 12
NA_BLOCKS_PER_STEP = 4
NA_HEADS_PER_STEP = 2
NA_RPB_ROWS = 2 * NA_KH - 1
NA_RPB_COLS = 2 * NA_KW - 1

ML_HEADS = 4
ML_HEAD_DIM = 256
ML_WIDTH = ML_HEADS * ML_HEAD_DIM
ML_CHUNK = 128
ML_CONV = 5
N_GATE = 4 * ML_HEADS

UNIT = 1024
U_G_NA, U_G_ML, U_Q_NA, U_K_NA, U_V_NA, U_Q_ML, U_K_ML, U_V_ML, U_O_ML = 0, 2, 4, 5, 6, 7, 8, 9, 10
N_UNITS = 11
GATE_PAD = 128

NEG_BIG = -1e30
VMEM_LIMIT = 60 * 1024 * 1024

F32 = jnp.float32
BF16 = jnp.bfloat16


def _cparams(n_grid_dims):
    return pltpu.CompilerParams(dimension_semantics=("arbitrary",) * n_grid_dims,
                                vmem_limit_bytes=VMEM_LIMIT)


def _sigmoid(x):
    return 1.0 / (1.0 + jnp.exp(-x))


def _layernorm_rows(y, g, b):
    mu = jnp.mean(y, axis=-1, keepdims=True)
    yc = y - mu
    var = jnp.mean(yc * yc, axis=-1, keepdims=True)
    return yc * lax.rsqrt(var + LN_EPS) * g + b


def _ffn_ln_kernel(x_ref, wg_ref, wu_ref, wd_ref, g_ref, b_ref, o_ref, xb_ref):
    j = pl.program_id(1)
    last = pl.num_programs(1) - 1
    tm = o_ref.shape[0]

    def partial_sum(rows):
        xb = xb_ref[rows, :]
        gate = jnp.dot(xb, wg_ref[...], preferred_element_type=F32)
        up = jnp.dot(xb, wu_ref[...], preferred_element_type=F32)
        act = gate * _sigmoid(gate) * up
        return jnp.dot(act.astype(BF16), wd_ref[...], preferred_element_type=F32)

    def chunks(n_rows):
        return [pl.ds(r * n_rows, n_rows) for r in range(tm // n_rows)]

    @pl.when(j == 0)
    def _():
        for rows in chunks(FFN_FIRST_ROWS):
            xb_ref[rows, :] = x_ref[rows, :].astype(BF16)
        for rows in chunks(FFN_FIRST_ROWS):
            o_ref[rows, :] = partial_sum(rows)

    @pl.when((j > 0) & (j < last))
    def _():
        o_ref[...] += partial_sum(slice(None))

    @pl.when(j == last)
    def _():
        def finish(rows, acc):
            y = ALPHA * x_ref[rows, :] + 0.5 * acc
            o_ref[rows, :] = _layernorm_rows(y, g_ref[...], b_ref[...])

        pending = None
        for rows in chunks(FFN_LAST_ROWS):
            acc = o_ref[rows, :] + partial_sum(rows)
            if pending is not None:
                finish(*pending)
            pending = (rows, acc)
        finish(*pending)


FFN_FIRST_ROWS = 512
FFN_LAST_ROWS = 256
FFN_TM = 1024
FFN_TF = 256


def _ffn_ln(x, w_gu, w_down, g, b):
    t, d = x.shape
    d_ff = w_down.shape[0]
    tm, tf = FFN_TM, FFN_TF
    n_ff = d_ff // tf
    assert t % tm == 0 and d_ff % tf == 0 and w_gu.shape == (d, 2 * d_ff) and n_ff >= 2
    return pl.pallas_call(
        _ffn_ln_kernel,
        grid=(t // tm, n_ff),
        in_specs=[
            pl.BlockSpec((tm, d), lambda i, j: (i, 0)),
            pl.BlockSpec((d, tf), lambda i, j: (0, j)),
            pl.BlockSpec((d, tf), lambda i, j: (0, j + n_ff)),
            pl.BlockSpec((tf, d), lambda i, j: (j, 0)),
            pl.BlockSpec((1, d), lambda i, j: (0, 0)),
            pl.BlockSpec((1, d), lambda i, j: (0, 0)),
        ],
        out_specs=pl.BlockSpec((tm, d), lambda i, j: (i, 0)),
        out_shape=jax.ShapeDtypeStruct((t, d), F32),
        scratch_shapes=[pltpu.VMEM((tm, d), BF16)],
        compiler_params=_cparams(2),
        name="ffn_ln",
    )(x, w_gu, w_gu, w_down, g, b)


PROJ_TM = 1024


def _in_proj_kernel(x_ref, w_ref, wgate_ref, z_ref, gates_ref, xb_ref):
    j = pl.program_id(1)

    @pl.when(j == 0)
    def _():
        xb_ref[...] = x_ref[...].astype(BF16)
        gates_ref[...] = jnp.dot(xb_ref[...], wgate_ref[...], preferred_element_type=F32)

    scale = jnp.where(j == U_Q_NA, NA_HEAD_DIM ** -0.5, 1.0)
    z = jnp.dot(xb_ref[...], w_ref[...], preferred_element_type=F32)
    z_ref[...] = (z * scale).astype(z_ref.dtype)


def _in_proj(x, w, w_gate):
    t, d = x.shape
    n_units = w.shape[1] // UNIT
    tm = PROJ_TM
    assert t % tm == 0 and w.shape[1] % UNIT == 0
    return pl.pallas_call(
        _in_proj_kernel,
        grid=(t // tm, n_units),
        in_specs=[
            pl.BlockSpec((tm, d), lambda i, j: (i, 0)),
            pl.BlockSpec((d, UNIT), lambda i, j: (0, j)),
            pl.BlockSpec((d, GATE_PAD), lambda i, j: (0, 0)),
        ],
        out_specs=[
            pl.BlockSpec((tm, UNIT), lambda i, j: (i, j)),
            pl.BlockSpec((tm, GATE_PAD), lambda i, j: (i, 0)),
        ],
        out_shape=[jax.ShapeDtypeStruct((t, n_units * UNIT), BF16),
                   jax.ShapeDtypeStruct((t, GATE_PAD), F32)],
        scratch_shapes=[pltpu.VMEM((tm, d), BF16)],
        compiler_params=_cparams(2),
        name="in_proj",
    )(x, w, w_gate)


def _na_bias_tables(rpb):
    heads = rpb.shape[0]
    c = np.arange(GRID_W)[:, None]
    kc = np.arange(GRID_W)[None, :]
    ws = np.clip(c - NA_KW // 2, 0, GRID_W - NA_KW)
    col_ok = (kc >= ws) & (kc < ws + NA_KW)
    period = 2 * GRID_W
    padded = jnp.pad(rpb.astype(F32), ((0, 0), (0, 0), (0, period - NA_RPB_COLS)))
    flat = jnp.broadcast_to(padded[:, :, None, :], (heads, NA_RPB_ROWS, GRID_W, period))
    flat = flat.reshape(heads, NA_RPB_ROWS, GRID_W * period)
    skew = flat[:, :, NA_KW - 1:NA_KW - 1 + GRID_W * (period - 1)]
    colb = skew.reshape(heads, NA_RPB_ROWS, GRID_W, period - 1)[..., :GRID_W]
    colb = jnp.where(jnp.asarray(col_ok)[None, None], colb, NEG_BIG)
    band_src = colb.transpose(0, 2, 1, 3).reshape(heads, GRID_W, NA_RPB_ROWS * GRID_W)

    tables = []
    for ty in range(3):
        per_query_row = []
        for a in range(NA_QROWS):
            if ty == 0:
                start, q_row = 0, a
            elif ty == 1:
                start, q_row = a, a + NA_KH // 2
            else:
                start, q_row = NA_KROWS - NA_KH, a + NA_KROWS - NA_QROWS
            first_rpb_row = start - q_row + NA_KH - 1
            assert 0 <= first_rpb_row and first_rpb_row + NA_KH <= NA_RPB_ROWS
            band = band_src[:, :, first_rpb_row * GRID_W:(first_rpb_row + NA_KH) * GRID_W]
            per_query_row.append(jnp.pad(
                band, ((0, 0), (0, 0), (start * GRID_W, (NA_KROWS - NA_KH - start) * GRID_W)),
                constant_values=NEG_BIG))
        tab = jnp.stack(per_query_row, axis=1)
        tables.append(tab.reshape(heads, NA_QROWS * GRID_W, NA_KROWS * GRID_W))
    return jnp.stack(tables)


def _na_kernel(q_ref, k_ref, v_ref, tab_first_ref, tab_mid_ref, tab_last_ref, o_ref, *, rows):
    i = pl.program_id(2)
    nq = NA_QROWS * GRID_W
    nk = NA_KROWS * GRID_W
    hd = NA_HEAD_DIM
    tab_refs = (tab_first_ref,) + (tab_mid_ref,) * (NA_BLOCKS_PER_STEP - 2) + (tab_last_ref,)
    problems = []
    for blk in range(NA_BLOCKS_PER_STEP):
        q_row0 = NA_QROWS * (NA_BLOCKS_PER_STEP * i + blk)
        first_key_row = jnp.clip(q_row0 - NA_KH // 2, 0, rows - NA_KROWS)
        start = pl.multiple_of(first_key_row * GRID_W, GRID_W * NA_QROWS)
        for head in range(NA_HEADS_PER_STEP):
            problems.append((blk, head, start))
    scores = []
    for blk, head, start in problems:
        lanes = slice(head * hd, (head + 1) * hd)
        q = q_ref[blk * nq:(blk + 1) * nq, lanes]
        k = k_ref[pl.ds(start, nk), lanes]
        s = lax.dot_general(q, k, (((1,), (1,)), ((), ())), preferred_element_type=F32)
        scores.append(s + tab_refs[blk][head])
    probs, sums = [], []
    for s in scores:
        m = jnp.max(s, axis=-1, keepdims=True)
        p = jnp.exp(s - m)
        sums.append(jnp.sum(p, axis=-1, keepdims=True))
        probs.append(p.astype(BF16))
    for (blk, head, start), p, l in zip(problems, probs, sums):
        lanes = slice(head * hd, (head + 1) * hd)
        v = v_ref[pl.ds(start, nk), lanes]
        o = jnp.dot(p, v, preferred_element_type=F32)
        o_ref[blk * nq:(blk + 1) * nq, lanes] = (o / l).astype(o_ref.dtype)


def _neighbourhood_attention(z, tables, *, batch, seq):
    rows = seq // GRID_W
    step_rows = NA_QROWS * NA_BLOCKS_PER_STEP
    assert NA_BLOCKS_PER_STEP >= 2 and NA_KROWS == NA_KH + NA_QROWS and NA_HEADS % NA_HEADS_PER_STEP == 0
    assert seq % GRID_W == 0 and rows % step_rows == 0 and rows >= 2 * NA_KROWS
    nstep = rows // step_rows
    nq_step = step_rows * GRID_W
    width = NA_HEADS_PER_STEP * NA_HEAD_DIM
    per_unit = UNIT // width
    tab_shape = (None, NA_HEADS_PER_STEP, NA_QROWS * GRID_W, NA_KROWS * GRID_W)

    return pl.pallas_call(
        functools.partial(_na_kernel, rows=rows),
        grid=(batch, NA_HEADS // NA_HEADS_PER_STEP, nstep),
        in_specs=[
            pl.BlockSpec((nq_step, width), lambda b, h, i: (b * nstep + i, U_Q_NA * per_unit + h)),
            pl.BlockSpec((seq, width), lambda b, h, i: (b, U_K_NA * per_unit + h)),
            pl.BlockSpec((seq, width), lambda b, h, i: (b, U_V_NA * per_unit + h)),
            pl.BlockSpec(tab_shape, lambda b, h, i: (jnp.where(i == 0, 0, 1), h, 0, 0)),
            pl.BlockSpec(tab_shape, lambda b, h, i: (1, h, 0, 0)),
            pl.BlockSpec(tab_shape, lambda b, h, i: (jnp.where(i == nstep - 1, 2, 1), h, 0, 0)),
        ],
        out_specs=pl.BlockSpec((nq_step, width), lambda b, h, i: (b * nstep + i, h)),
        out_shape=jax.ShapeDtypeStruct((batch * seq, NA_WIDTH), BF16),
        compiler_params=_cparams(3),
        name="na",
    )(z, z, z, tables, tables, tables)


HALO = 64
CONV_ROWS = 128


CONV_SHIFTS = tuple(j for j in range(ML_CONV) if j != ML_CONV // 2)


def _conv_shift_matrix():
    pad = ML_CONV // 2
    sel = np.zeros((len(CONV_SHIFTS), CONV_ROWS, CONV_ROWS + 2 * HALO), np.float32)
    for i, j in enumerate(CONV_SHIFTS):
        sel[i, np.arange(CONV_ROWS), np.arange(CONV_ROWS) + HALO + j - pad] = 1.0
    return jnp.asarray(sel.reshape(len(CONV_SHIFTS) * CONV_ROWS, CONV_ROWS + 2 * HALO), BF16)


def _conv_silu_kernel(prev_ref, cur_ref, next_ref, sel_ref, w_ref, b_ref, o_ref, ext_ref, *, tc):
    t = pl.program_id(1)
    part = pl.program_id(2)
    zero = jnp.zeros((HALO, cur_ref.shape[1]), ext_ref.dtype)
    ext_ref[0:HALO, :] = jnp.where(t == 0, zero, prev_ref[...])
    ext_ref[HALO:HALO + tc, :] = cur_ref[...]
    ext_ref[HALO + tc:, :] = jnp.where(t == pl.num_programs(1) - 1, zero, next_ref[...])
    scale = jnp.where(part == 1, ML_HEAD_DIM ** -0.5, 1.0)
    for blk in range(tc // CONV_ROWS):
        window = ext_ref[blk * CONV_ROWS:(blk + 1) * CONV_ROWS + 2 * HALO, :]
        shifted = jnp.dot(sel_ref[...], window, preferred_element_type=F32)
        centre = ext_ref[blk * CONV_ROWS + HALO:(blk + 1) * CONV_ROWS + HALO, :].astype(F32)
        pad = ML_CONV // 2
        y = centre * w_ref[pad:pad + 1, :] + b_ref[...]
        for i, j in enumerate(CONV_SHIFTS):
            y = y + shifted[i * CONV_ROWS:(i + 1) * CONV_ROWS, :] * w_ref[j:j + 1, :]
        y = y * _sigmoid(y)
        o_ref[blk * CONV_ROWS:(blk + 1) * CONV_ROWS, :] = (y * scale).astype(o_ref.dtype)


def _conv_silu(z, conv_w, conv_b, *, batch, seq, tc=512):
    assert seq % tc == 0 and tc % HALO == 0 and tc % CONV_ROWS == 0 and U_K_ML == U_Q_ML + 1
    nt = seq // tc
    per = tc // HALO
    last_halo = batch * seq // HALO - 1
    sel = _conv_shift_matrix()

    return pl.pallas_call(
        functools.partial(_conv_silu_kernel, tc=tc),
        grid=(batch, nt, 2),
        in_specs=[
            pl.BlockSpec((HALO, UNIT), lambda b, t, p: (jnp.maximum((b * nt + t) * per - 1, 0), U_Q_ML + p)),
            pl.BlockSpec((tc, UNIT), lambda b, t, p: (b * nt + t, U_Q_ML + p)),
            pl.BlockSpec((HALO, UNIT), lambda b, t, p: (jnp.minimum((b * nt + t + 1) * per, last_halo), U_Q_ML + p)),
            pl.BlockSpec(sel.shape, lambda b, t, p: (0, 0)),
            pl.BlockSpec((ML_CONV, UNIT), lambda b, t, p: (0, p)),
            pl.BlockSpec((1, UNIT), lambda b, t, p: (0, p)),
        ],
        out_specs=pl.BlockSpec((tc, UNIT), lambda b, t, p: (b * nt + t, p)),
        out_shape=jax.ShapeDtypeStruct((batch * seq, 2 * ML_WIDTH), BF16),
        scratch_shapes=[pltpu.VMEM((tc + 2 * HALO, UNIT), BF16)],
        compiler_params=_cparams(3),
        name="conv_silu",
    )(z, z, z, sel, conv_w, conv_b)


def _log_sigmoid(x):
    return jnp.minimum(x, 0.0) - jnp.log(1.0 + jnp.exp(-jnp.abs(x)))


class _Stream:
    pass


def _split_bf16(x, parts):
    out = []
    for _ in range(parts):
        piece = x.astype(BF16)
        out.append(piece)
        x = x - piece.astype(F32)
    return out


def _mlstm_kernel(qf_ref, kf_ref, vf_ref, growf_ref, qb_ref, kb_ref, vb_ref, growb_ref,
                  brow_ref, hf_ref, hb_ref, c_ref, n_ref, m_ref):
    @pl.when(pl.program_id(1) == 0)
    def _():
        c_ref[...] = jnp.zeros_like(c_ref)
        n_ref[...] = jnp.zeros_like(n_ref)
        m_ref[...] = jnp.zeros_like(m_ref)

    L = ML_CHUNK
    dh = ML_HEAD_DIM
    nh = ML_HEADS
    t_idx = lax.broadcasted_iota(jnp.int32, (L, L), 0)
    s_idx = lax.broadcasted_iota(jnp.int32, (L, L), 1)
    per_direction = (
        (qf_ref, kf_ref, vf_ref, growf_ref, hf_ref, s_idx <= t_idx, t_idx <= s_idx),
        (qb_ref, kb_ref, vb_ref, growb_ref, hb_ref, s_idx >= t_idx, t_idx >= s_idx),
    )

    streams = []
    for direction, (q_ref, k_ref, v_ref, grow_ref, h_ref, seen, seen_t) in enumerate(per_direction):
        seen_f = seen.astype(F32)
        g_row = grow_ref[...] + brow_ref[direction]
        i_rows = g_row[:nh]
        f_rows = _log_sigmoid(g_row[nh:])
        pieces = _split_bf16(f_rows, 3) + [jnp.zeros((nh, L), BF16)]
        cums = jnp.dot(jnp.concatenate(pieces, axis=0), seen_t.astype(BF16), preferred_element_type=F32)
        b_rows = cums[:nh] + cums[nh:2 * nh] + cums[2 * nh:3 * nh]
        for h in range(nh):
            sl = slice(h * dh, (h + 1) * dh)
            s = _Stream()
            s.state, s.sl, s.h_ref, s.seen = direction * nh + h, sl, h_ref, seen
            s.q, s.k, s.v = q_ref[:, sl], k_ref[:, sl], v_ref[:, sl]
            f_row = f_rows[h:h + 1]
            s.i_row = i_rows[h:h + 1]
            s.b_row = b_rows[h:h + 1]
            s.b_col = jnp.sum(seen_f * f_row, axis=1, keepdims=True)
            s.b_last = jnp.sum(f_row, axis=1, keepdims=True)
            streams.append(s)
    for s in streams:
        s.w_row = s.i_row - s.b_row
        s.wmax = jnp.max(jnp.where(s.seen, s.w_row, NEG_BIG), axis=1, keepdims=True)
    for s in streams:
        s.m_prev = m_ref[s.state]
        s.c_prev = c_ref[s.state]
        s.n_prev = n_ref[s.state]
        kn = jnp.concatenate([s.k, jnp.broadcast_to(s.n_prev.astype(BF16), (L, dh))], axis=0)
        qkn = lax.dot_general(s.q, kn, (((1,), (1,)), ((), ())), preferred_element_type=F32)
        s.qk = qkn[:, :L]
        s.qn = qkn[:, L:L + 1]
        s.qc = jnp.dot(s.q, s.c_prev.astype(BF16), preferred_element_type=F32)
    for s in streams:
        s.c_col = jnp.maximum(s.m_prev, s.wmax)
        s.sc = s.qk * jnp.exp(jnp.where(s.seen, s.w_row - s.c_col, NEG_BIG))
        s.a_inter = jnp.exp(s.m_prev - s.c_col)
    for s in streams:
        s.num = jnp.dot(s.sc.astype(BF16), s.v, preferred_element_type=F32) + s.a_inter * s.qc
        s.den = jnp.sum(s.sc, axis=1, keepdims=True) + s.a_inter * s.qn
    for s in streams:
        floor = jnp.exp(-(s.b_col + s.c_col))
        s.h_ref[:, s.sl] = (s.num / jnp.maximum(jnp.abs(s.den), floor)).astype(s.h_ref.dtype)
    for s in streams:
        g_end = s.b_last + s.w_row
        s.m_new = jnp.maximum(s.b_last + s.m_prev, jnp.max(g_end, axis=1, keepdims=True))
        s.e_row = jnp.exp(g_end - s.m_new)
        s.decay = jnp.exp(s.b_last + s.m_prev - s.m_new)
    for s in streams:
        wk_t = (s.k.astype(F32).T * s.e_row).astype(BF16)
        e_pieces = _split_bf16(s.e_row, 2) + [jnp.zeros((6, L), BF16)]
        ksum = jnp.dot(jnp.concatenate(e_pieces, axis=0), s.k, preferred_element_type=F32)
        c_ref[s.state] = s.decay * s.c_prev + jnp.dot(wk_t, s.v, preferred_element_type=F32)
        n_ref[s.state] = s.decay * s.n_prev + ksum[0:1] + ksum[1:2]
        m_ref[s.state] = s.m_new


def _mlstm(qk, z, gates_row, bias_row, *, batch, seq):
    assert seq % ML_CHUNK == 0
    nc = seq // ML_CHUNK
    L = ML_CHUNK
    n_state = 2 * ML_HEADS
    fwd = lambda b, c: b * nc + c
    bwd = lambda b, c: b * nc + nc - 1 - c

    def chunk_specs(direction, tok):
        return [
            pl.BlockSpec((L, ML_WIDTH), lambda b, c: (tok(b, c), 0)),
            pl.BlockSpec((L, ML_WIDTH), lambda b, c: (tok(b, c), 1)),
            pl.BlockSpec((L, UNIT), lambda b, c: (tok(b, c), U_V_ML)),
            pl.BlockSpec((None, 2 * ML_HEADS, L), lambda b, c: (direction, 0, tok(b, c))),
        ]

    out_sds = jax.ShapeDtypeStruct((batch * seq, ML_WIDTH), BF16)
    return pl.pallas_call(
        _mlstm_kernel,
        grid=(batch, nc),
        in_specs=chunk_specs(0, fwd) + chunk_specs(1, bwd) + [
            pl.BlockSpec((2, 2 * ML_HEADS, 1), lambda b, c: (0, 0, 0)),
        ],
        out_specs=[
            pl.BlockSpec((L, ML_WIDTH), lambda b, c: (fwd(b, c), 0)),
            pl.BlockSpec((L, ML_WIDTH), lambda b, c: (bwd(b, c), 0)),
        ],
        out_shape=[out_sds, out_sds],
        scratch_shapes=[
            pltpu.VMEM((n_state, ML_HEAD_DIM, ML_HEAD_DIM), F32),
            pltpu.VMEM((n_state, 1, ML_HEAD_DIM), F32),
            pltpu.VMEM((n_state, 1, 1), F32),
        ],
        compiler_params=_cparams(2),
        name="mlstm",
    )(qk, qk, z, gates_row, qk, qk, z, gates_row, bias_row)


def _merge_ln_kernel(x_ref, hf_ref, hb_ref, o_ref, ya_ref, ga_ref, gm_ref, ng_ref,
                     wpa_ref, wpm_ref, wout_ref, g_ref, b_ref, out_ref):
    h = hf_ref[...].astype(F32) + hb_ref[...].astype(F32)
    parts = []
    for hh in range(ML_HEADS):
        sl = slice(hh * ML_HEAD_DIM, (hh + 1) * ML_HEAD_DIM)
        hs = h[:, sl]
        mu = jnp.mean(hs, axis=-1, keepdims=True)
        hc = hs - mu
        var = jnp.mean(hc * hc, axis=-1, keepdims=True)
        parts.append(hc * lax.rsqrt(var + LN_EPS))
    y_m = jnp.concatenate(parts, axis=-1) * ng_ref[...] * _sigmoid(o_ref[...].astype(F32))
    pa = jnp.dot(ya_ref[...], wpa_ref[...], preferred_element_type=F32)
    pm = jnp.dot(y_m.astype(BF16), wpm_ref[...], preferred_element_type=F32)
    merged = _sigmoid(ga_ref[...].astype(F32)) * pa + _sigmoid(gm_ref[...].astype(F32)) * pm
    mix = jnp.dot(merged.astype(BF16), wout_ref[...], preferred_element_type=F32)
    out_ref[...] = _layernorm_rows(ALPHA * x_ref[...] + mix, g_ref[...], b_ref[...])


def _merge_ln(x, h_fwd, h_bwd, z, y_a, norm_g, w_pa, w_pm, w_out, g, b, *, tm=256):
    t, d = x.shape
    assert t % tm == 0 and U_G_NA % 2 == 0 and U_G_ML % 2 == 0
    wide = 2 * UNIT
    const = lambda i: (0, 0)
    return pl.pallas_call(
        _merge_ln_kernel,
        grid=(t // tm,),
        in_specs=[
            pl.BlockSpec((tm, d), lambda i: (i, 0)),
            pl.BlockSpec((tm, ML_WIDTH), lambda i: (i, 0)),
            pl.BlockSpec((tm, ML_WIDTH), lambda i: (i, 0)),
            pl.BlockSpec((tm, UNIT), lambda i: (i, U_O_ML)),
            pl.BlockSpec((tm, NA_WIDTH), lambda i: (i, 0)),
            pl.BlockSpec((tm, wide), lambda i: (i, U_G_NA // 2)),
            pl.BlockSpec((tm, wide), lambda i: (i, U_G_ML // 2)),
            pl.BlockSpec((1, ML_WIDTH), const),
            pl.BlockSpec((NA_WIDTH, d), const, pipeline_mode=pl.Buffered(1)),
            pl.BlockSpec((ML_WIDTH, d), const, pipeline_mode=pl.Buffered(1)),
            pl.BlockSpec((d, d), const, pipeline_mode=pl.Buffered(1)),
            pl.BlockSpec((1, d), const),
            pl.BlockSpec((1, d), const),
        ],
        out_specs=pl.BlockSpec((tm, d), lambda i: (i, 0)),
        out_shape=jax.ShapeDtypeStruct((t, d), F32),
        compiler_params=_cparams(1),
        name="merge_ln",
    )(x, h_fwd, h_bwd, z, y_a, z, z, norm_g, w_pa, w_pm, w_out, g, b)


def _prepare_params(ffa_w_gu, ffa_w_down, norm_a_g, norm_a_b, mix_w_in, na_rpb, ml_conv_w, ml_conv_b,
                    ml_gate_b, ml_norm_g, mix_w_pa, mix_w_pm, mix_w_out, norm_m_g, norm_m_b,
                    ffb_w_gu, ffb_w_down, norm_b_g, norm_b_b, l):
    sizes = (NA_WIDTH, NA_WIDTH, NA_WIDTH, 2 * ML_WIDTH, ML_WIDTH, ML_WIDTH, N_GATE, D_MODEL, D_MODEL)
    points = np.cumsum(sizes)[:-1].tolist()
    w_qa, w_ka, w_va, w_qkm, w_vm, w_om, w_gates, w_ga, w_gm = jnp.split(
        mix_w_in[l].astype(BF16), points, axis=1)
    w_in = jnp.concatenate([w_ga, w_gm, w_qa, w_ka, w_va, w_qkm, w_vm, w_om], axis=1)
    assert w_in.shape[1] == N_UNITS * UNIT
    w_gate = jnp.pad(w_gates, ((0, 0), (0, GATE_PAD - N_GATE)))
    gb = ml_gate_b[l].astype(F32).reshape(2, 2, ML_HEADS).transpose(1, 0, 2).reshape(2, 2 * ML_HEADS)
    row = lambda v: v.reshape(1, -1).astype(F32)
    return dict(
        ffa_w_gu=ffa_w_gu[l].astype(BF16), ffa_w_down=ffa_w_down[l].astype(BF16),
        norm_a_g=row(norm_a_g[l]), norm_a_b=row(norm_a_b[l]),
        w_in=w_in, w_gate=w_gate,
        na_tables=_na_bias_tables(na_rpb[l]),
        conv_w=ml_conv_w[l].astype(F32), conv_b=row(ml_conv_b[l]),
        gate_b_row=gb.reshape(2, 2 * ML_HEADS, 1),
        ml_norm_g=row(ml_norm_g[l]),
        w_pa=mix_w_pa[l].astype(BF16), w_pm=mix_w_pm[l].astype(BF16), w_out=mix_w_out[l].astype(BF16),
        norm_m_g=row(norm_m_g[l]), norm_m_b=row(norm_m_b[l]),
        ffb_w_gu=ffb_w_gu[l].astype(BF16), ffb_w_down=ffb_w_down[l].astype(BF16),
        norm_b_g=row(norm_b_g[l]), norm_b_b=row(norm_b_b[l]),
    )


def _encoder_layer(x, p):
    batch, seq, d = x.shape
    t = batch * seq
    x0 = x.reshape(t, d)
    x1 = _ffn_ln(x0, p["ffa_w_gu"], p["ffa_w_down"], p["norm_a_g"], p["norm_a_b"])
    z, gates = _in_proj(x1, p["w_in"], p["w_gate"])
    y_a = _neighbourhood_attention(z, p["na_tables"], batch=batch, seq=seq)
    qk = _conv_silu(z, p["conv_w"], p["conv_b"], batch=batch, seq=seq)
    gd = gates[:, :N_GATE].reshape(t, 2, 2, ML_HEADS)
    gates_row = gd.transpose(2, 1, 3, 0).reshape(2, 2 * ML_HEADS, t)
    h_fwd, h_bwd = _mlstm(qk, z, gates_row, p["gate_b_row"], batch=batch, seq=seq)
    x2 = _merge_ln(x1, h_fwd, h_bwd, z, y_a, p["ml_norm_g"], p["w_pa"], p["w_pm"], p["w_out"],
                   p["norm_m_g"], p["norm_m_b"])
    y = _ffn_ln(x2, p["ffb_w_gu"], p["ffb_w_down"], p["norm_b_g"], p["norm_b_b"])
    return y.reshape(batch, seq, d)


def kernel(x_prompt, x_sample, ffa_w_gu, ffa_w_down, norm_a_g, norm_a_b, mix_w_in, na_rpb, ml_conv_w, ml_conv_b, ml_gate_b, ml_norm_g, mix_w_pa, mix_w_pm, mix_w_out, norm_m_g, norm_m_b, ffb_w_gu, ffb_w_down, norm_b_g, norm_b_b):
    assert ffa_w_gu.shape[0] == DEPTH
    y_prompt, y_sample = x_prompt, x_sample
    for l in range(DEPTH):
        p = _prepare_params(ffa_w_gu, ffa_w_down, norm_a_g, norm_a_b, mix_w_in, na_rpb, ml_conv_w,
                            ml_conv_b, ml_gate_b, ml_norm_g, mix_w_pa, mix_w_pm, mix_w_out, norm_m_g,
                            norm_m_b, ffb_w_gu, ffb_w_down, norm_b_g, norm_b_b, l)
        y_prompt = _encoder_layer(y_prompt, p)
        y_sample = _encoder_layer(y_sample, p)
    return (y_prompt, y_sample)
```

```python
import functools

import jax
import jax.numpy as jnp
import numpy as np
from jax import lax
from jax.experimental import pallas as pl
from jax.experimental.pallas import tpu as pltpu

D_MODEL = 2048
D_FF = 5632
DEPTH = 1
ALPHA = (2 * DEPTH) ** 0.25
LN_EPS = 1e-5

GRID_W = 64
NA_HEADS = 8
NA_HEAD_DIM = 128
NA_WIDTH = NA_HEADS * NA_HEAD_DIM
NA_KH = 8
NA_KW = 16
NA_QROWS = 4
NA_KROWS = 12
NA_BLOCKS_PER_STEP = 4
NA_HEADS_PER_STEP = 2
NA_RPB_ROWS = 2 * NA_KH - 1
NA_RPB_COLS = 2 * NA_KW - 1

ML_HEADS = 4
ML_HEAD_DIM = 256
ML_WIDTH = ML_HEADS * ML_HEAD_DIM
ML_CHUNK = 128
ML_CONV = 5
N_GATE = 4 * ML_HEADS

UNIT = 1024
U_G_NA, U_G_ML, U_Q_NA, U_K_NA, U_V_NA, U_Q_ML, U_K_ML, U_V_ML, U_O_ML = 0, 2, 4, 5, 6, 7, 8, 9, 10
N_UNITS = 11
GATE_PAD = 128

NEG_BIG = -1e30
V7X_VMEM_BYTES = 64 * 1024 * 1024
VMEM_LIMIT = V7X_VMEM_BYTES - 1024 * 1024

F32 = jnp.float32
BF16 = jnp.bfloat16


def _cparams(n_grid_dims):
    return pltpu.CompilerParams(dimension_semantics=("arbitrary",) * n_grid_dims,
                                vmem_limit_bytes=VMEM_LIMIT)


def _sigmoid(x):
    return 1.0 / (1.0 + jnp.exp(-x))


def _layernorm_rows(y, g, b):
    mu = jnp.mean(y, axis=-1, keepdims=True)
    yc = y - mu
    var = jnp.mean(yc * yc, axis=-1, keepdims=True)
    return yc * lax.rsqrt(var + LN_EPS) * g + b


def _ffn_ln_kernel(x_ref, wg_ref, wu_ref, wd_ref, g_ref, b_ref, o_ref, xb_ref):
    j = pl.program_id(1)
    last = pl.num_programs(1) - 1
    tm = o_ref.shape[0]

    def partial_sum(rows):
        xb = xb_ref[rows, :]
        gate = jnp.dot(xb, wg_ref[...], preferred_element_type=F32)
        up = jnp.dot(xb, wu_ref[...], preferred_element_type=F32)
        act = gate * _sigmoid(gate) * up
        return jnp.dot(act.astype(BF16), wd_ref[...], preferred_element_type=F32)

    def chunks(n_rows):
        return [pl.ds(r * n_rows, n_rows) for r in range(tm // n_rows)]

    @pl.when(j == 0)
    def _():
        for rows in chunks(FFN_FIRST_ROWS):
            xb_ref[rows, :] = x_ref[rows, :].astype(BF16)
        for rows in chunks(FFN_FIRST_ROWS):
            o_ref[rows, :] = partial_sum(rows)

    @pl.when((j > 0) & (j < last))
    def _():
        o_ref[...] += partial_sum(slice(None))

    @pl.when(j == last)
    def _():
        def finish(rows, acc):
            y = ALPHA * x_ref[rows, :] + 0.5 * acc
            o_ref[rows, :] = _layernorm_rows(y, g_ref[...], b_ref[...])

        pending = None
        for rows in chunks(FFN_LAST_ROWS):
            acc = o_ref[rows, :] + partial_sum(rows)
            if pending is not None:
                finish(*pending)
            pending = (rows, acc)
        finish(*pending)


FFN_FIRST_ROWS = 512
FFN_LAST_ROWS = 256
FFN_TM = 1024
FFN_TF = 512


def _ffn_ln(x, w_gu, w_down, g, b):
    t, d = x.shape
    d_ff = w_down.shape[0]
    tm, tf = FFN_TM, FFN_TF
    n_ff = d_ff // tf
    assert t % tm == 0 and d_ff % tf == 0 and w_gu.shape == (d, 2 * d_ff) and n_ff >= 2
    return pl.pallas_call(
        _ffn_ln_kernel,
        grid=(t // tm, n_ff),
        in_specs=[
            pl.BlockSpec((tm, d), lambda i, j: (i, 0)),
            pl.BlockSpec((d, tf), lambda i, j: (0, j)),
            pl.BlockSpec((d, tf), lambda i, j: (0, j + n_ff)),
            pl.BlockSpec((tf, d), lambda i, j: (j, 0)),
            pl.BlockSpec((1, d), lambda i, j: (0, 0)),
            pl.BlockSpec((1, d), lambda i, j: (0, 0)),
        ],
        out_specs=pl.BlockSpec((tm, d), lambda i, j: (i, 0)),
        out_shape=jax.ShapeDtypeStruct((t, d), F32),
        scratch_shapes=[pltpu.VMEM((tm, d), BF16)],
        compiler_params=_cparams(2),
        name="ffn_ln",
    )(x, w_gu, w_gu, w_down, g, b)


PROJ_TM = 1024


def _in_proj_kernel(x_ref, w_ref, wgate_ref, z_ref, gates_ref, xb_ref):
    j = pl.program_id(1)

    @pl.when(j == 0)
    def _():
        xb_ref[...] = x_ref[...].astype(BF16)
        gates_ref[...] = jnp.dot(xb_ref[...], wgate_ref[...], preferred_element_type=F32)

    scale = jnp.where(j == U_Q_NA, NA_HEAD_DIM ** -0.5, 1.0)
    z = jnp.dot(xb_ref[...], w_ref[...], preferred_element_type=F32)
    z_ref[...] = (z * scale).astype(z_ref.dtype)


def _in_proj(x, w, w_gate):
    t, d = x.shape
    n_units = w.shape[1] // UNIT
    tm = PROJ_TM
    assert t % tm == 0 and w.shape[1] % UNIT == 0
    return pl.pallas_call(
        _in_proj_kernel,
        grid=(t // tm, n_units),
        in_specs=[
            pl.BlockSpec((tm, d), lambda i, j: (i, 0)),
            pl.BlockSpec((d, UNIT), lambda i, j: (0, j)),
            pl.BlockSpec((d, GATE_PAD), lambda i, j: (0, 0)),
        ],
        out_specs=[
            pl.BlockSpec((tm, UNIT), lambda i, j: (i, j)),
            pl.BlockSpec((tm, GATE_PAD), lambda i, j: (i, 0)),
        ],
        out_shape=[jax.ShapeDtypeStruct((t, n_units * UNIT), BF16),
                   jax.ShapeDtypeStruct((t, GATE_PAD), F32)],
        scratch_shapes=[pltpu.VMEM((tm, d), BF16)],
        compiler_params=_cparams(2),
        name="in_proj",
    )(x, w, w_gate)


def _na_bias_tables(rpb):
    heads = rpb.shape[0]
    c = np.arange(GRID_W)[:, None]
    kc = np.arange(GRID_W)[None, :]
    ws = np.clip(c - NA_KW // 2, 0, GRID_W - NA_KW)
    col_ok = (kc >= ws) & (kc < ws + NA_KW)
    period = 2 * GRID_W
    padded = jnp.pad(rpb.astype(F32), ((0, 0), (0, 0), (0, period - NA_RPB_COLS)))
    flat = jnp.broadcast_to(padded[:, :, None, :], (heads, NA_RPB_ROWS, GRID_W, period))
    flat = flat.reshape(heads, NA_RPB_ROWS, GRID_W * period)
    skew = flat[:, :, NA_KW - 1:NA_KW - 1 + GRID_W * (period - 1)]
    colb = skew.reshape(heads, NA_RPB_ROWS, GRID_W, period - 1)[..., :GRID_W]
    colb = jnp.where(jnp.asarray(col_ok)[None, None], colb, NEG_BIG)
    band_src = colb.transpose(0, 2, 1, 3).reshape(heads, GRID_W, NA_RPB_ROWS * GRID_W)

    tables = []
    for ty in range(3):
        per_query_row = []
        for a in range(NA_QROWS):
            if ty == 0:
                start, q_row = 0, a
            elif ty == 1:
                start, q_row = a, a + NA_KH // 2
            else:
                start, q_row = NA_KROWS - NA_KH, a + NA_KROWS - NA_QROWS
            first_rpb_row = start - q_row + NA_KH - 1
            assert 0 <= first_rpb_row and first_rpb_row + NA_KH <= NA_RPB_ROWS
            band = band_src[:, :, first_rpb_row * GRID_W:(first_rpb_row + NA_KH) * GRID_W]
            per_query_row.append(jnp.pad(
                band, ((0, 0), (0, 0), (start * GRID_W, (NA_KROWS - NA_KH - start) * GRID_W)),
                constant_values=NEG_BIG))
        tab = jnp.stack(per_query_row, axis=1)
        tables.append(tab.reshape(heads, NA_QROWS * GRID_W, NA_KROWS * GRID_W))
    return jnp.stack(tables)


def _na_kernel(q_ref, k_ref, v_ref, tab_first_ref, tab_mid_ref, tab_last_ref, o_ref, *, rows):
    i = pl.program_id(2)
    nq = NA_QROWS * GRID_W
    nk = NA_KROWS * GRID_W
    hd = NA_HEAD_DIM
    tab_refs = (tab_first_ref,) + (tab_mid_ref,) * (NA_BLOCKS_PER_STEP - 2) + (tab_last_ref,)
    problems = []
    for blk in range(NA_BLOCKS_PER_STEP):
        q_row0 = NA_QROWS * (NA_BLOCKS_PER_STEP * i + blk)
        first_key_row = jnp.clip(q_row0 - NA_KH // 2, 0, rows - NA_KROWS)
        start = pl.multiple_of(first_key_row * GRID_W, GRID_W * NA_QROWS)
        for head in range(NA_HEADS_PER_STEP):
            problems.append((blk, head, start))
    scores = []
    for blk, head, start in problems:
        lanes = slice(head * hd, (head + 1) * hd)
        q = q_ref[blk * nq:(blk + 1) * nq, lanes]
        k = k_ref[pl.ds(start, nk), lanes]
        s = lax.dot_general(q, k, (((1,), (1,)), ((), ())), preferred_element_type=F32)
        scores.append(s + tab_refs[blk][head])
    probs, sums = [], []
    for s in scores:
        m = jnp.max(s, axis=-1, keepdims=True)
        p = jnp.exp(s - m)
        sums.append(jnp.sum(p, axis=-1, keepdims=True))
        probs.append(p.astype(BF16))
    for (blk, head, start), p, l in zip(problems, probs, sums):
        lanes = slice(head * hd, (head + 1) * hd)
        v = v_ref[pl.ds(start, nk), lanes]
        o = jnp.dot(p, v, preferred_element_type=F32)
        o_ref[blk * nq:(blk + 1) * nq, lanes] = (o / l).astype(o_ref.dtype)


def _neighbourhood_attention(z, tables, *, batch, seq):
    rows = seq // GRID_W
    step_rows = NA_QROWS * NA_BLOCKS_PER_STEP
    assert NA_BLOCKS_PER_STEP >= 2 and NA_KROWS == NA_KH + NA_QROWS and NA_HEADS % NA_HEADS_PER_STEP == 0
    assert seq % GRID_W == 0 and rows % step_rows == 0 and rows >= 2 * NA_KROWS
    nstep = rows // step_rows
    nq_step = step_rows * GRID_W
    width = NA_HEADS_PER_STEP * NA_HEAD_DIM
    per_unit = UNIT // width
    tab_shape = (None, NA_HEADS_PER_STEP, NA_QROWS * GRID_W, NA_KROWS * GRID_W)

    return pl.pallas_call(
        functools.partial(_na_kernel, rows=rows),
        grid=(batch, NA_HEADS // NA_HEADS_PER_STEP, nstep),
        in_specs=[
            pl.BlockSpec((nq_step, width), lambda b, h, i: (b * nstep + i, U_Q_NA * per_unit + h)),
            pl.BlockSpec((seq, width), lambda b, h, i: (b, U_K_NA * per_unit + h)),
            pl.BlockSpec((seq, width), lambda b, h, i: (b, U_V_NA * per_unit + h)),
            pl.BlockSpec(tab_shape, lambda b, h, i: (jnp.where(i == 0, 0, 1), h, 0, 0)),
            pl.BlockSpec(tab_shape, lambda b, h, i: (1, h, 0, 0)),
            pl.BlockSpec(tab_shape, lambda b, h, i: (jnp.where(i == nstep - 1, 2, 1), h, 0, 0)),
        ],
        out_specs=pl.BlockSpec((nq_step, width), lambda b, h, i: (b * nstep + i, h)),
        out_shape=jax.ShapeDtypeStruct((batch * seq, NA_WIDTH), BF16),
        compiler_params=_cparams(3),
        name="na",
    )(z, z, z, tables, tables, tables)


HALO = 64
CONV_ROWS = 128


CONV_SHIFTS = tuple(j for j in range(ML_CONV) if j != ML_CONV // 2)


def _conv_shift_matrix():
    pad = ML_CONV // 2
    sel = np.zeros((len(CONV_SHIFTS), CONV_ROWS, CONV_ROWS + 2 * HALO), np.float32)
    for i, j in enumerate(CONV_SHIFTS):
        sel[i, np.arange(CONV_ROWS), np.arange(CONV_ROWS) + HALO + j - pad] = 1.0
    return jnp.asarray(sel.reshape(len(CONV_SHIFTS) * CONV_ROWS, CONV_ROWS + 2 * HALO), BF16)


def _conv_silu_kernel(prev_ref, cur_ref, next_ref, sel_ref, w_ref, b_ref, o_ref, ext_ref, *, tc):
    t = pl.program_id(1)
    part = pl.program_id(2)
    zero = jnp.zeros((HALO, cur_ref.shape[1]), ext_ref.dtype)
    ext_ref[0:HALO, :] = jnp.where(t == 0, zero, prev_ref[...])
    ext_ref[HALO:HALO + tc, :] = cur_ref[...]
    ext_ref[HALO + tc:, :] = jnp.where(t == pl.num_programs(1) - 1, zero, next_ref[...])
    scale = jnp.where(part == 1, ML_HEAD_DIM ** -0.5, 1.0)
    for blk in range(tc // CONV_ROWS):
        window = ext_ref[blk * CONV_ROWS:(blk + 1) * CONV_ROWS + 2 * HALO, :]
        shifted = jnp.dot(sel_ref[...], window, preferred_element_type=F32)
        centre = ext_ref[blk * CONV_ROWS + HALO:(blk + 1) * CONV_ROWS + HALO, :].astype(F32)
        pad = ML_CONV // 2
        y = centre * w_ref[pad:pad + 1, :] + b_ref[...]
        for i, j in enumerate(CONV_SHIFTS):
            y = y + shifted[i * CONV_ROWS:(i + 1) * CONV_ROWS, :] * w_ref[j:j + 1, :]
        y = y * _sigmoid(y)
        o_ref[blk * CONV_ROWS:(blk + 1) * CONV_ROWS, :] = (y * scale).astype(o_ref.dtype)


def _conv_silu(z, conv_w, conv_b, *, batch, seq, tc=1024):
    assert seq % tc == 0 and tc % HALO == 0 and tc % CONV_ROWS == 0 and U_K_ML == U_Q_ML + 1
    nt = seq // tc
    per = tc // HALO
    last_halo = batch * seq // HALO - 1
    sel = _conv_shift_matrix()

    return pl.pallas_call(
        functools.partial(_conv_silu_kernel, tc=tc),
        grid=(batch, nt, 2),
        in_specs=[
            pl.BlockSpec((HALO, UNIT), lambda b, t, p: (jnp.maximum((b * nt + t) * per - 1, 0), U_Q_ML + p)),
            pl.BlockSpec((tc, UNIT), lambda b, t, p: (b * nt + t, U_Q_ML + p)),
            pl.BlockSpec((HALO, UNIT), lambda b, t, p: (jnp.minimum((b * nt + t + 1) * per, last_halo), U_Q_ML + p)),
            pl.BlockSpec(sel.shape, lambda b, t, p: (0, 0)),
            pl.BlockSpec((ML_CONV, UNIT), lambda b, t, p: (0, p)),
            pl.BlockSpec((1, UNIT), lambda b, t, p: (0, p)),
        ],
        out_specs=pl.BlockSpec((tc, UNIT), lambda b, t, p: (b * nt + t, p)),
        out_shape=jax.ShapeDtypeStruct((batch * seq, 2 * ML_WIDTH), BF16),
        scratch_shapes=[pltpu.VMEM((tc + 2 * HALO, UNIT), BF16)],
        compiler_params=_cparams(3),
        name="conv_silu",
    )(z, z, z, sel, conv_w, conv_b)


def _log_sigmoid(x):
    return jnp.minimum(x, 0.0) - jnp.log(1.0 + jnp.exp(-jnp.abs(x)))


class _Stream:
    pass


def _split_bf16(x, parts):
    out = []
    for _ in range(parts):
        piece = x.astype(BF16)
        out.append(piece)
        x = x - piece.astype(F32)
    return out


def _mlstm_kernel(qf_ref, kf_ref, vf_ref, growf_ref, qb_ref, kb_ref, vb_ref, growb_ref,
                  brow_ref, hf_ref, hb_ref, c_ref, n_ref, m_ref):
    @pl.when(pl.program_id(1) == 0)
    def _():
        c_ref[...] = jnp.zeros_like(c_ref)
        n_ref[...] = jnp.zeros_like(n_ref)
        m_ref[...] = jnp.zeros_like(m_ref)

    L = ML_CHUNK
    dh = ML_HEAD_DIM
    nh = ML_HEADS
    t_idx = lax.broadcasted_iota(jnp.int32, (L, L), 0)
    s_idx = lax.broadcasted_iota(jnp.int32, (L, L), 1)
    per_direction = (
        (qf_ref, kf_ref, vf_ref, growf_ref, hf_ref, s_idx <= t_idx, t_idx <= s_idx),
        (qb_ref, kb_ref, vb_ref, growb_ref, hb_ref, s_idx >= t_idx, t_idx >= s_idx),
    )

    streams = []
    for direction, (q_ref, k_ref, v_ref, grow_ref, h_ref, seen, seen_t) in enumerate(per_direction):
        seen_f = seen.astype(F32)
        g_row = grow_ref[...] + brow_ref[direction]
        i_rows = g_row[:nh]
        f_rows = _log_sigmoid(g_row[nh:])
        pieces = _split_bf16(f_rows, 3) + [jnp.zeros((nh, L), BF16)]
        cums = jnp.dot(jnp.concatenate(pieces, axis=0), seen_t.astype(BF16), preferred_element_type=F32)
        b_rows = cums[:nh] + cums[nh:2 * nh] + cums[2 * nh:3 * nh]
        for h in range(nh):
            sl = slice(h * dh, (h + 1) * dh)
            s = _Stream()
            s.state, s.sl, s.h_ref, s.seen = direction * nh + h, sl, h_ref, seen
            s.q, s.k, s.v = q_ref[:, sl], k_ref[:, sl], v_ref[:, sl]
            f_row = f_rows[h:h + 1]
            s.i_row = i_rows[h:h + 1]
            s.b_row = b_rows[h:h + 1]
            s.b_col = jnp.sum(seen_f * f_row, axis=1, keepdims=True)
            s.b_last = jnp.sum(f_row, axis=1, keepdims=True)
            streams.append(s)
    for s in streams:
        s.w_row = s.i_row - s.b_row
        s.wmax = jnp.max(jnp.where(s.seen, s.w_row, NEG_BIG), axis=1, keepdims=True)
    for s in streams:
        s.m_prev = m_ref[s.state]
        s.c_prev = c_ref[s.state]
        s.n_prev = n_ref[s.state]
        kn = jnp.concatenate([s.k, jnp.broadcast_to(s.n_prev.astype(BF16), (L, dh))], axis=0)
        qkn = lax.dot_general(s.q, kn, (((1,), (1,)), ((), ())), preferred_element_type=F32)
        s.qk = qkn[:, :L]
        s.qn = qkn[:, L:L + 1]
        s.qc = jnp.dot(s.q, s.c_prev.astype(BF16), preferred_element_type=F32)
    for s in streams:
        s.c_col = jnp.maximum(s.m_prev, s.wmax)
        s.sc = s.qk * jnp.exp(jnp.where(s.seen, s.w_row - s.c_col, NEG_BIG))
        s.a_inter = jnp.exp(s.m_prev - s.c_col)
    for s in streams:
        s.num = jnp.dot(s.sc.astype(BF16), s.v, preferred_element_type=F32) + s.a_inter * s.qc
        s.den = jnp.sum(s.sc, axis=1, keepdims=True) + s.a_inter * s.qn
    for s in streams:
        floor = jnp.exp(-(s.b_col + s.c_col))
        s.h_ref[:, s.sl] = (s.num / jnp.maximum(jnp.abs(s.den), floor)).astype(s.h_ref.dtype)
    for s in streams:
        g_end = s.b_last + s.w_row
        s.m_new = jnp.maximum(s.b_last + s.m_prev, jnp.max(g_end, axis=1, keepdims=True))
        s.e_row = jnp.exp(g_end - s.m_new)
        s.decay = jnp.exp(s.b_last + s.m_prev - s.m_new)
    for s in streams:
        wk_t = (s.k.astype(F32).T * s.e_row).astype(BF16)
        e_pieces = _split_bf16(s.e_row, 2) + [jnp.zeros((6, L), BF16)]
        ksum = jnp.dot(jnp.concatenate(e_pieces, axis=0), s.k, preferred_element_type=F32)
        c_ref[s.state] = s.decay * s.c_prev + jnp.dot(wk_t, s.v, preferred_element_type=F32)
        n_ref[s.state] = s.decay * s.n_prev + ksum[0:1] + ksum[1:2]
        m_ref[s.state] = s.m_new


def _mlstm(qk, z, gates_row, bias_row, *, batch, seq):
    assert seq % ML_CHUNK == 0
    nc = seq // ML_CHUNK
    L = ML_CHUNK
    n_state = 2 * ML_HEADS
    fwd = lambda b, c: b * nc + c
    bwd = lambda b, c: b * nc + nc - 1 - c

    def chunk_specs(direction, tok):
        return [
            pl.BlockSpec((L, ML_WIDTH), lambda b, c: (tok(b, c), 0)),
            pl.BlockSpec((L, ML_WIDTH), lambda b, c: (tok(b, c), 1)),
            pl.BlockSpec((L, UNIT), lambda b, c: (tok(b, c), U_V_ML)),
            pl.BlockSpec((None, 2 * ML_HEADS, L), lambda b, c: (direction, 0, tok(b, c))),
        ]

    out_sds = jax.ShapeDtypeStruct((batch * seq, ML_WIDTH), BF16)
    return pl.pallas_call(
        _mlstm_kernel,
        grid=(batch, nc),
        in_specs=chunk_specs(0, fwd) + chunk_specs(1, bwd) + [
            pl.BlockSpec((2, 2 * ML_HEADS, 1), lambda b, c: (0, 0, 0)),
        ],
        out_specs=[
            pl.BlockSpec((L, ML_WIDTH), lambda b, c: (fwd(b, c), 0)),
            pl.BlockSpec((L, ML_WIDTH), lambda b, c: (bwd(b, c), 0)),
        ],
        out_shape=[out_sds, out_sds],
        scratch_shapes=[
            pltpu.VMEM((n_state, ML_HEAD_DIM, ML_HEAD_DIM), F32),
            pltpu.VMEM((n_state, 1, ML_HEAD_DIM), F32),
            pltpu.VMEM((n_state, 1, 1), F32),
        ],
        compiler_params=_cparams(2),
        name="mlstm",
    )(qk, qk, z, gates_row, qk, qk, z, gates_row, bias_row)


def _merge_ln_kernel(x_ref, hf_ref, hb_ref, o_ref, ya_ref, ga_ref, gm_ref, ng_ref,
                     wpa_ref, wpm_ref, wout_ref, g_ref, b_ref, out_ref):
    h = hf_ref[...].astype(F32) + hb_ref[...].astype(F32)
    parts = []
    for hh in range(ML_HEADS):
        sl = slice(hh * ML_HEAD_DIM, (hh + 1) * ML_HEAD_DIM)
        hs = h[:, sl]
        mu = jnp.mean(hs, axis=-1, keepdims=True)
        hc = hs - mu
        var = jnp.mean(hc * hc, axis=-1, keepdims=True)
        parts.append(hc * lax.rsqrt(var + LN_EPS))
    y_m = jnp.concatenate(parts, axis=-1) * ng_ref[...] * _sigmoid(o_ref[...].astype(F32))
    pa = jnp.dot(ya_ref[...], wpa_ref[...], preferred_element_type=F32)
    pm = jnp.dot(y_m.astype(BF16), wpm_ref[...], preferred_element_type=F32)
    merged = _sigmoid(ga_ref[...].astype(F32)) * pa + _sigmoid(gm_ref[...].astype(F32)) * pm
    mix = jnp.dot(merged.astype(BF16), wout_ref[...], preferred_element_type=F32)
    out_ref[...] = _layernorm_rows(ALPHA * x_ref[...] + mix, g_ref[...], b_ref[...])


def _merge_ln(x, h_fwd, h_bwd, z, y_a, norm_g, w_pa, w_pm, w_out, g, b, *, tm=512):
    t, d = x.shape
    assert t % tm == 0 and U_G_NA % 2 == 0 and U_G_ML % 2 == 0
    wide = 2 * UNIT
    const = lambda i: (0, 0)
    return pl.pallas_call(
        _merge_ln_kernel,
        grid=(t // tm,),
        in_specs=[
            pl.BlockSpec((tm, d), lambda i: (i, 0)),
            pl.BlockSpec((tm, ML_WIDTH), lambda i: (i, 0)),
            pl.BlockSpec((tm, ML_WIDTH), lambda i: (i, 0)),
            pl.BlockSpec((tm, UNIT), lambda i: (i, U_O_ML)),
            pl.BlockSpec((tm, NA_WIDTH), lambda i: (i, 0)),
            pl.BlockSpec((tm, wide), lambda i: (i, U_G_NA // 2)),
            pl.BlockSpec((tm, wide), lambda i: (i, U_G_ML // 2)),
            pl.BlockSpec((1, ML_WIDTH), const),
            pl.BlockSpec((NA_WIDTH, d), const, pipeline_mode=pl.Buffered(1)),
            pl.BlockSpec((ML_WIDTH, d), const, pipeline_mode=pl.Buffered(1)),
            pl.BlockSpec((d, d), const, pipeline_mode=pl.Buffered(1)),
            pl.BlockSpec((1, d), const),
            pl.BlockSpec((1, d), const),
        ],
        out_specs=pl.BlockSpec((tm, d), lambda i: (i, 0)),
        out_shape=jax.ShapeDtypeStruct((t, d), F32),
        compiler_params=_cparams(1),
        name="merge_ln",
    )(x, h_fwd, h_bwd, z, y_a, z, z, norm_g, w_pa, w_pm, w_out, g, b)


def _prepare_params(ffa_w_gu, ffa_w_down, norm_a_g, norm_a_b, mix_w_in, na_rpb, ml_conv_w, ml_conv_b,
                    ml_gate_b, ml_norm_g, mix_w_pa, mix_w_pm, mix_w_out, norm_m_g, norm_m_b,
                    ffb_w_gu, ffb_w_down, norm_b_g, norm_b_b, l):
    sizes = (NA_WIDTH, NA_WIDTH, NA_WIDTH, 2 * ML_WIDTH, ML_WIDTH, ML_WIDTH, N_GATE, D_MODEL, D_MODEL)
    points = np.cumsum(sizes)[:-1].tolist()
    w_qa, w_ka, w_va, w_qkm, w_vm, w_om, w_gates, w_ga, w_gm = jnp.split(
        mix_w_in[l].astype(BF16), points, axis=1)
    w_in = jnp.concatenate([w_ga, w_gm, w_qa, w_ka, w_va, w_qkm, w_vm, w_om], axis=1)
    assert w_in.shape[1] == N_UNITS * UNIT
    w_gate = jnp.pad(w_gates, ((0, 0), (0, GATE_PAD - N_GATE)))
    gb = ml_gate_b[l].astype(F32).reshape(2, 2, ML_HEADS).transpose(1, 0, 2).reshape(2, 2 * ML_HEADS)
    row = lambda v: v.reshape(1, -1).astype(F32)
    return dict(
        ffa_w_gu=ffa_w_gu[l].astype(BF16), ffa_w_down=ffa_w_down[l].astype(BF16),
        norm_a_g=row(norm_a_g[l]), norm_a_b=row(norm_a_b[l]),
        w_in=w_in, w_gate=w_gate,
        na_tables=_na_bias_tables(na_rpb[l]),
        conv_w=ml_conv_w[l].astype(F32), conv_b=row(ml_conv_b[l]),
        gate_b_row=gb.reshape(2, 2 * ML_HEADS, 1),
        ml_norm_g=row(ml_norm_g[l]),
        w_pa=mix_w_pa[l].astype(BF16), w_pm=mix_w_pm[l].astype(BF16), w_out=mix_w_out[l].astype(BF16),
        norm_m_g=row(norm_m_g[l]), norm_m_b=row(norm_m_b[l]),
        ffb_w_gu=ffb_w_gu[l].astype(BF16), ffb_w_down=ffb_w_down[l].astype(BF16),
        norm_b_g=row(norm_b_g[l]), norm_b_b=row(norm_b_b[l]),
    )


def _encoder_layer(x, p):
    batch, seq, d = x.shape
    t = batch * seq
    x0 = x.reshape(t, d)
    x1 = _ffn_ln(x0, p["ffa_w_gu"], p["ffa_w_down"], p["norm_a_g"], p["norm_a_b"])
    z, gates = _in_proj(x1, p["w_in"], p["w_gate"])
    y_a = _neighbourhood_attention(z, p["na_tables"], batch=batch, seq=seq)
    qk = _conv_silu(z, p["conv_w"], p["conv_b"], batch=batch, seq=seq)
    gd = gates[:, :N_GATE].reshape(t, 2, 2, ML_HEADS)
    gates_row = gd.transpose(2, 1, 3, 0).reshape(2, 2 * ML_HEADS, t)
    h_fwd, h_bwd = _mlstm(qk, z, gates_row, p["gate_b_row"], batch=batch, seq=seq)
    x2 = _merge_ln(x1, h_fwd, h_bwd, z, y_a, p["ml_norm_g"], p["w_pa"], p["w_pm"], p["w_out"],
                   p["norm_m_g"], p["norm_m_b"])
    y = _ffn_ln(x2, p["ffb_w_gu"], p["ffb_w_down"], p["norm_b_g"], p["norm_b_b"])
    return y.reshape(batch, seq, d)


def kernel(x_prompt, x_sample, ffa_w_gu, ffa_w_down, norm_a_g, norm_a_b, mix_w_in, na_rpb, ml_conv_w, ml_conv_b, ml_gate_b, ml_norm_g, mix_w_pa, mix_w_pm, mix_w_out, norm_m_g, norm_m_b, ffb_w_gu, ffb_w_down, norm_b_g, norm_b_b):
    assert ffa_w_gu.shape[0] == DEPTH
    y_prompt, y_sample = x_prompt, x_sample
    for l in range(DEPTH):
        p = _prepare_params(ffa_w_gu, ffa_w_down, norm_a_g, norm_a_b, mix_w_in, na_rpb, ml_conv_w,
                            ml_conv_b, ml_gate_b, ml_norm_g, mix_w_pa, mix_w_pm, mix_w_out, norm_m_g,
                            norm_m_b, ffb_w_gu, ffb_w_down, norm_b_g, norm_b_b, l)
        y_prompt = _encoder_layer(y_prompt, p)
        y_sample = _encoder_layer(y_sample, p)
    return (y_prompt, y_sample)
```

```python
import functools

import jax
import jax.numpy as jnp
import numpy as np
from jax import lax
from jax.experimental import pallas as pl
from jax.experimental.pallas import tpu as pltpu

D_MODEL = 2048
D_FF = 5632
DEPTH = 1
ALPHA = (2 * DEPTH) ** 0.25
LN_EPS = 1e-5

GRID_W = 64
NA_HEADS = 8
NA_HEAD_DIM = 128
NA_WIDTH = NA_HEADS * NA_HEAD_DIM
NA_KH = 8
NA_KW = 16
NA_QROWS = 4
NA_KROWS = 12
NA_BLOCKS_PER_STEP = 4
NA_HEADS_PER_STEP = 2
NA_RPB_ROWS = 2 * NA_KH - 1
NA_RPB_COLS = 2 * NA_KW - 1

ML_HEADS = 4
ML_HEAD_DIM = 256
ML_WIDTH = ML_HEADS * ML_HEAD_DIM
ML_CHUNK = 128
ML_CHUNKS_PER_STEP = 2
ML_CONV = 5
N_GATE = 4 * ML_HEADS

UNIT = 1024
U_G_NA, U_G_ML, U_Q_NA, U_K_NA, U_V_NA, U_Q_ML, U_K_ML, U_V_ML, U_O_ML = 0, 2, 4, 5, 6, 7, 8, 9, 10
N_UNITS = 11
GATE_PAD = 128

NEG_BIG = -1e30
V7X_VMEM_BYTES = 64 * 1024 * 1024
VMEM_LIMIT = V7X_VMEM_BYTES - 1024 * 1024

F32 = jnp.float32
BF16 = jnp.bfloat16


def _cparams(n_grid_dims):
    return pltpu.CompilerParams(dimension_semantics=("arbitrary",) * n_grid_dims,
                                vmem_limit_bytes=VMEM_LIMIT)


def _sigmoid(x):
    return 1.0 / (1.0 + jnp.exp(-x))


def _layernorm_rows(y, g, b):
    mu = jnp.mean(y, axis=-1, keepdims=True)
    yc = y - mu
    var = jnp.mean(yc * yc, axis=-1, keepdims=True)
    return yc * lax.rsqrt(var + LN_EPS) * g + b


def _ffn_ln_kernel(x_ref, wg_ref, wu_ref, wd_ref, g_ref, b_ref, o_ref, xb_ref):
    j = pl.program_id(1)
    last = pl.num_programs(1) - 1
    tm = o_ref.shape[0]

    def partial_sum(rows):
        xb = xb_ref[rows, :]
        gate = jnp.dot(xb, wg_ref[...], preferred_element_type=F32)
        up = jnp.dot(xb, wu_ref[...], preferred_element_type=F32)
        act = gate * _sigmoid(gate) * up
        return jnp.dot(act.astype(BF16), wd_ref[...], preferred_element_type=F32)

    def chunks(n_rows):
        return [pl.ds(r * n_rows, n_rows) for r in range(tm // n_rows)]

    @pl.when(j == 0)
    def _():
        for rows in chunks(FFN_FIRST_ROWS):
            xb_ref[rows, :] = x_ref[rows, :].astype(BF16)
        for rows in chunks(FFN_FIRST_ROWS):
            o_ref[rows, :] = partial_sum(rows)

    @pl.when((j > 0) & (j < last))
    def _():
        o_ref[...] += partial_sum(slice(None))

    @pl.when(j == last)
    def _():
        def finish(rows, acc):
            y = ALPHA * x_ref[rows, :] + 0.5 * acc
            o_ref[rows, :] = _layernorm_rows(y, g_ref[...], b_ref[...])

        pending = None
        for rows in chunks(FFN_LAST_ROWS):
            acc = o_ref[rows, :] + partial_sum(rows)
            if pending is not None:
                finish(*pending)
            pending = (rows, acc)
        finish(*pending)


FFN_FIRST_ROWS = 512
FFN_LAST_ROWS = 256
FFN_TM = 1024
FFN_TF = 512


def _ffn_ln(x, w_gu, w_down, g, b):
    t, d = x.shape
    d_ff = w_down.shape[0]
    tm, tf = FFN_TM, FFN_TF
    n_ff = d_ff // tf
    assert t % tm == 0 and d_ff % tf == 0 and w_gu.shape == (d, 2 * d_ff) and n_ff >= 2
    return pl.pallas_call(
        _ffn_ln_kernel,
        grid=(t // tm, n_ff),
        in_specs=[
            pl.BlockSpec((tm, d), lambda i, j: (i, 0)),
            pl.BlockSpec((d, tf), lambda i, j: (0, j)),
            pl.BlockSpec((d, tf), lambda i, j: (0, j + n_ff)),
            pl.BlockSpec((tf, d), lambda i, j: (j, 0)),
            pl.BlockSpec((1, d), lambda i, j: (0, 0)),
            pl.BlockSpec((1, d), lambda i, j: (0, 0)),
        ],
        out_specs=pl.BlockSpec((tm, d), lambda i, j: (i, 0)),
        out_shape=jax.ShapeDtypeStruct((t, d), F32),
        scratch_shapes=[pltpu.VMEM((tm, d), BF16)],
        compiler_params=_cparams(2),
        name="ffn_ln",
    )(x, w_gu, w_gu, w_down, g, b)


PROJ_TM = 1024


def _in_proj_kernel(x_ref, w_ref, wgate_ref, z_ref, gates_ref, xb_ref):
    j = pl.program_id(1)

    @pl.when(j == 0)
    def _():
        xb_ref[...] = x_ref[...].astype(BF16)
        gates_ref[...] = jnp.dot(xb_ref[...], wgate_ref[...], preferred_element_type=F32)

    scale = jnp.where(j == U_Q_NA, NA_HEAD_DIM ** -0.5, 1.0)
    z = jnp.dot(xb_ref[...], w_ref[...], preferred_element_type=F32)
    z_ref[...] = (z * scale).astype(z_ref.dtype)


def _in_proj(x, w, w_gate):
    t, d = x.shape
    n_units = w.shape[1] // UNIT
    tm = PROJ_TM
    assert t % tm == 0 and w.shape[1] % UNIT == 0
    return pl.pallas_call(
        _in_proj_kernel,
        grid=(t // tm, n_units),
        in_specs=[
            pl.BlockSpec((tm, d), lambda i, j: (i, 0)),
            pl.BlockSpec((d, UNIT), lambda i, j: (0, j)),
            pl.BlockSpec((d, GATE_PAD), lambda i, j: (0, 0)),
        ],
        out_specs=[
            pl.BlockSpec((tm, UNIT), lambda i, j: (i, j)),
            pl.BlockSpec((tm, GATE_PAD), lambda i, j: (i, 0)),
        ],
        out_shape=[jax.ShapeDtypeStruct((t, n_units * UNIT), BF16),
                   jax.ShapeDtypeStruct((t, GATE_PAD), F32)],
        scratch_shapes=[pltpu.VMEM((tm, d), BF16)],
        compiler_params=_cparams(2),
        name="in_proj",
    )(x, w, w_gate)


def _na_bias_tables(rpb):
    heads = rpb.shape[0]
    c = np.arange(GRID_W)[:, None]
    kc = np.arange(GRID_W)[None, :]
    ws = np.clip(c - NA_KW // 2, 0, GRID_W - NA_KW)
    col_ok = (kc >= ws) & (kc < ws + NA_KW)
    period = 2 * GRID_W
    padded = jnp.pad(rpb.astype(F32), ((0, 0), (0, 0), (0, period - NA_RPB_COLS)))
    flat = jnp.broadcast_to(padded[:, :, None, :], (heads, NA_RPB_ROWS, GRID_W, period))
    flat = flat.reshape(heads, NA_RPB_ROWS, GRID_W * period)
    skew = flat[:, :, NA_KW - 1:NA_KW - 1 + GRID_W * (period - 1)]
    colb = skew.reshape(heads, NA_RPB_ROWS, GRID_W, period - 1)[..., :GRID_W]
    colb = jnp.where(jnp.asarray(col_ok)[None, None], colb, NEG_BIG)
    band_src = colb.transpose(0, 2, 1, 3).reshape(heads, GRID_W, NA_RPB_ROWS * GRID_W)

    tables = []
    for ty in range(3):
        per_query_row = []
        for a in range(NA_QROWS):
            if ty == 0:
                start, q_row = 0, a
            elif ty == 1:
                start, q_row = a, a + NA_KH // 2
            else:
                start, q_row = NA_KROWS - NA_KH, a + NA_KROWS - NA_QROWS
            first_rpb_row = start - q_row + NA_KH - 1
            assert 0 <= first_rpb_row and first_rpb_row + NA_KH <= NA_RPB_ROWS
            band = band_src[:, :, first_rpb_row * GRID_W:(first_rpb_row + NA_KH) * GRID_W]
            per_query_row.append(jnp.pad(
                band, ((0, 0), (0, 0), (start * GRID_W, (NA_KROWS - NA_KH - start) * GRID_W)),
                constant_values=NEG_BIG))
        tab = jnp.stack(per_query_row, axis=1)
        tables.append(tab.reshape(heads, NA_QROWS * GRID_W, NA_KROWS * GRID_W))
    return jnp.stack(tables)


def _na_kernel(q_ref, k_ref, v_ref, tab_first_ref, tab_mid_ref, tab_last_ref, o_ref, *, rows):
    i = pl.program_id(2)
    nq = NA_QROWS * GRID_W
    nk = NA_KROWS * GRID_W
    hd = NA_HEAD_DIM
    tab_refs = (tab_first_ref,) + (tab_mid_ref,) * (NA_BLOCKS_PER_STEP - 2) + (tab_last_ref,)
    problems = []
    for blk in range(NA_BLOCKS_PER_STEP):
        q_row0 = NA_QROWS * (NA_BLOCKS_PER_STEP * i + blk)
        first_key_row = jnp.clip(q_row0 - NA_KH // 2, 0, rows - NA_KROWS)
        start = pl.multiple_of(first_key_row * GRID_W, GRID_W * NA_QROWS)
        for head in range(NA_HEADS_PER_STEP):
            problems.append((blk, head, start))
    scores = []
    for blk, head, start in problems:
        lanes = slice(head * hd, (head + 1) * hd)
        q = q_ref[blk * nq:(blk + 1) * nq, lanes]
        k = k_ref[pl.ds(start, nk), lanes]
        s = lax.dot_general(q, k, (((1,), (1,)), ((), ())), preferred_element_type=F32)
        scores.append(s + tab_refs[blk][head])
    probs, sums = [], []
    for s in scores:
        m = jnp.max(s, axis=-1, keepdims=True)
        p = jnp.exp(s - m)
        sums.append(jnp.sum(p, axis=-1, keepdims=True))
        probs.append(p.astype(BF16))
    for (blk, head, start), p, l in zip(problems, probs, sums):
        lanes = slice(head * hd, (head + 1) * hd)
        v = v_ref[pl.ds(start, nk), lanes]
        o = jnp.dot(p, v, preferred_element_type=F32)
        o_ref[blk * nq:(blk + 1) * nq, lanes] = (o / l).astype(o_ref.dtype)


def _neighbourhood_attention(z, tables, *, batch, seq):
    rows = seq // GRID_W
    step_rows = NA_QROWS * NA_BLOCKS_PER_STEP
    assert NA_BLOCKS_PER_STEP >= 2 and NA_KROWS == NA_KH + NA_QROWS and NA_HEADS % NA_HEADS_PER_STEP == 0
    assert seq % GRID_W == 0 and rows % step_rows == 0 and rows >= 2 * NA_KROWS
    nstep = rows // step_rows
    nq_step = step_rows * GRID_W
    width = NA_HEADS_PER_STEP * NA_HEAD_DIM
    per_unit = UNIT // width
    tab_shape = (None, NA_HEADS_PER_STEP, NA_QROWS * GRID_W, NA_KROWS * GRID_W)

    return pl.pallas_call(
        functools.partial(_na_kernel, rows=rows),
        grid=(batch, NA_HEADS // NA_HEADS_PER_STEP, nstep),
        in_specs=[
            pl.BlockSpec((nq_step, width), lambda b, h, i: (b * nstep + i, U_Q_NA * per_unit + h)),
            pl.BlockSpec((seq, width), lambda b, h, i: (b, U_K_NA * per_unit + h)),
            pl.BlockSpec((seq, width), lambda b, h, i: (b, U_V_NA * per_unit + h)),
            pl.BlockSpec(tab_shape, lambda b, h, i: (jnp.where(i == 0, 0, 1), h, 0, 0)),
            pl.BlockSpec(tab_shape, lambda b, h, i: (1, h, 0, 0)),
            pl.BlockSpec(tab_shape, lambda b, h, i: (jnp.where(i == nstep - 1, 2, 1), h, 0, 0)),
        ],
        out_specs=pl.BlockSpec((nq_step, width), lambda b, h, i: (b * nstep + i, h)),
        out_shape=jax.ShapeDtypeStruct((batch * seq, NA_WIDTH), BF16),
        compiler_params=_cparams(3),
        name="na",
    )(z, z, z, tables, tables, tables)


HALO = 64
CONV_ROWS = 128


CONV_SHIFTS = tuple(j for j in range(ML_CONV) if j != ML_CONV // 2)


def _conv_shift_matrix():
    pad = ML_CONV // 2
    sel = np.zeros((len(CONV_SHIFTS), CONV_ROWS, CONV_ROWS + 2 * HALO), np.float32)
    for i, j in enumerate(CONV_SHIFTS):
        sel[i, np.arange(CONV_ROWS), np.arange(CONV_ROWS) + HALO + j - pad] = 1.0
    return jnp.asarray(sel.reshape(len(CONV_SHIFTS) * CONV_ROWS, CONV_ROWS + 2 * HALO), BF16)


def _conv_silu_kernel(prev_ref, cur_ref, next_ref, sel_ref, w_ref, b_ref, o_ref, ext_ref, *, tc):
    t = pl.program_id(1)
    part = pl.program_id(2)
    zero = jnp.zeros((HALO, cur_ref.shape[1]), ext_ref.dtype)
    ext_ref[0:HALO, :] = jnp.where(t == 0, zero, prev_ref[...])
    ext_ref[HALO:HALO + tc, :] = cur_ref[...]
    ext_ref[HALO + tc:, :] = jnp.where(t == pl.num_programs(1) - 1, zero, next_ref[...])
    scale = jnp.where(part == 1, ML_HEAD_DIM ** -0.5, 1.0)
    for blk in range(tc // CONV_ROWS):
        window = ext_ref[blk * CONV_ROWS:(blk + 1) * CONV_ROWS + 2 * HALO, :]
        shifted = jnp.dot(sel_ref[...], window, preferred_element_type=F32)
        centre = ext_ref[blk * CONV_ROWS + HALO:(blk + 1) * CONV_ROWS + HALO, :].astype(F32)
        pad = ML_CONV // 2
        y = centre * w_ref[pad:pad + 1, :] + b_ref[...]
        for i, j in enumerate(CONV_SHIFTS):
            y = y + shifted[i * CONV_ROWS:(i + 1) * CONV_ROWS, :] * w_ref[j:j + 1, :]
        y = y * _sigmoid(y)
        o_ref[blk * CONV_ROWS:(blk + 1) * CONV_ROWS, :] = (y * scale).astype(o_ref.dtype)


def _conv_silu(z, conv_w, conv_b, *, batch, seq, tc=1024):
    assert seq % tc == 0 and tc % HALO == 0 and tc % CONV_ROWS == 0 and U_K_ML == U_Q_ML + 1
    nt = seq // tc
    per = tc // HALO
    last_halo = batch * seq // HALO - 1
    sel = _conv_shift_matrix()

    return pl.pallas_call(
        functools.partial(_conv_silu_kernel, tc=tc),
        grid=(batch, nt, 2),
        in_specs=[
            pl.BlockSpec((HALO, UNIT), lambda b, t, p: (jnp.maximum((b * nt + t) * per - 1, 0), U_Q_ML + p)),
            pl.BlockSpec((tc, UNIT), lambda b, t, p: (b * nt + t, U_Q_ML + p)),
            pl.BlockSpec((HALO, UNIT), lambda b, t, p: (jnp.minimum((b * nt + t + 1) * per, last_halo), U_Q_ML + p)),
            pl.BlockSpec(sel.shape, lambda b, t, p: (0, 0)),
            pl.BlockSpec((ML_CONV, UNIT), lambda b, t, p: (0, p)),
            pl.BlockSpec((1, UNIT), lambda b, t, p: (0, p)),
        ],
        out_specs=pl.BlockSpec((tc, UNIT), lambda b, t, p: (b * nt + t, p)),
        out_shape=jax.ShapeDtypeStruct((batch * seq, 2 * ML_WIDTH), BF16),
        scratch_shapes=[pltpu.VMEM((tc + 2 * HALO, UNIT), BF16)],
        compiler_params=_cparams(3),
        name="conv_silu",
    )(z, z, z, sel, conv_w, conv_b)


def _log_sigmoid(x):
    return jnp.minimum(x, 0.0) - jnp.log(1.0 + jnp.exp(-jnp.abs(x)))


class _Stream:
    pass


def _split_bf16(x, parts):
    out = []
    for _ in range(parts):
        piece = x.astype(BF16)
        out.append(piece)
        x = x - piece.astype(F32)
    return out


def _mlstm_kernel(qf_ref, kf_ref, vf_ref, growf_ref, qb_ref, kb_ref, vb_ref, growb_ref,
                  brow_ref, hf_ref, hb_ref, c_ref, n_ref, m_ref):
    @pl.when(pl.program_id(1) == 0)
    def _():
        c_ref[...] = jnp.zeros_like(c_ref)
        n_ref[...] = jnp.zeros_like(n_ref)
        m_ref[...] = jnp.zeros_like(m_ref)

    for sub in range(ML_CHUNKS_PER_STEP):
        rows_f = pl.ds(sub * ML_CHUNK, ML_CHUNK)
        rows_b = pl.ds((ML_CHUNKS_PER_STEP - 1 - sub) * ML_CHUNK, ML_CHUNK)
        _mlstm_chunk_pair(rows_f, rows_b, qf_ref, kf_ref, vf_ref, growf_ref, qb_ref, kb_ref, vb_ref,
                          growb_ref, brow_ref, hf_ref, hb_ref, c_ref, n_ref, m_ref)


def _mlstm_chunk_pair(rows_f, rows_b, qf_ref, kf_ref, vf_ref, growf_ref, qb_ref, kb_ref, vb_ref,
                      growb_ref, brow_ref, hf_ref, hb_ref, c_ref, n_ref, m_ref):
    L = ML_CHUNK
    dh = ML_HEAD_DIM
    nh = ML_HEADS
    t_idx = lax.broadcasted_iota(jnp.int32, (L, L), 0)
    s_idx = lax.broadcasted_iota(jnp.int32, (L, L), 1)
    per_direction = (
        (qf_ref, kf_ref, vf_ref, growf_ref, hf_ref, rows_f, s_idx <= t_idx, t_idx <= s_idx),
        (qb_ref, kb_ref, vb_ref, growb_ref, hb_ref, rows_b, s_idx >= t_idx, t_idx >= s_idx),
    )

    streams = []
    for direction, (q_ref, k_ref, v_ref, grow_ref, h_ref, rows, seen, seen_t) in enumerate(per_direction):
        seen_f = seen.astype(F32)
        g_row = grow_ref[:, rows] + brow_ref[direction]
        i_rows = g_row[:nh]
        f_rows = _log_sigmoid(g_row[nh:])
        pieces = _split_bf16(f_rows, 3) + [jnp.zeros((nh, L), BF16)]
        cums = jnp.dot(jnp.concatenate(pieces, axis=0), seen_t.astype(BF16), preferred_element_type=F32)
        b_rows = cums[:nh] + cums[nh:2 * nh] + cums[2 * nh:3 * nh]
        for h in range(nh):
            sl = slice(h * dh, (h + 1) * dh)
            s = _Stream()
            s.state, s.sl, s.rows, s.h_ref, s.seen = direction * nh + h, sl, rows, h_ref, seen
            s.q, s.k, s.v = q_ref[rows, sl], k_ref[rows, sl], v_ref[rows, sl]
            f_row = f_rows[h:h + 1]
            s.i_row = i_rows[h:h + 1]
            s.b_row = b_rows[h:h + 1]
            s.b_col = jnp.sum(seen_f * f_row, axis=1, keepdims=True)
            s.b_last = jnp.sum(f_row, axis=1, keepdims=True)
            streams.append(s)
    for s in streams:
        s.w_row = s.i_row - s.b_row
        s.wmax = jnp.max(jnp.where(s.seen, s.w_row, NEG_BIG), axis=1, keepdims=True)
    for s in streams:
        s.m_prev = m_ref[s.state]
        s.c_prev = c_ref[s.state]
        s.n_prev = n_ref[s.state]
        kn = jnp.concatenate([s.k, jnp.broadcast_to(s.n_prev.astype(BF16), (L, dh))], axis=0)
        qkn = lax.dot_general(s.q, kn, (((1,), (1,)), ((), ())), preferred_element_type=F32)
        s.qk = qkn[:, :L]
        s.qn = qkn[:, L:L + 1]
        s.qc = jnp.dot(s.q, s.c_prev.astype(BF16), preferred_element_type=F32)
    for s in streams:
        s.c_col = jnp.maximum(s.m_prev, s.wmax)
        s.sc = s.qk * jnp.exp(jnp.where(s.seen, s.w_row - s.c_col, NEG_BIG))
        s.a_inter = jnp.exp(s.m_prev - s.c_col)
    for s in streams:
        s.num = jnp.dot(s.sc.astype(BF16), s.v, preferred_element_type=F32) + s.a_inter * s.qc
        s.den = jnp.sum(s.sc, axis=1, keepdims=True) + s.a_inter * s.qn
    for s in streams:
        floor = jnp.exp(-(s.b_col + s.c_col))
        s.h_ref[s.rows, s.sl] = (s.num / jnp.maximum(jnp.abs(s.den), floor)).astype(s.h_ref.dtype)
    for s in streams:
        g_end = s.b_last + s.w_row
        s.m_new = jnp.maximum(s.b_last + s.m_prev, jnp.max(g_end, axis=1, keepdims=True))
        s.e_row = jnp.exp(g_end - s.m_new)
        s.decay = jnp.exp(s.b_last + s.m_prev - s.m_new)
    for s in streams:
        wk_t = (s.k.astype(F32).T * s.e_row).astype(BF16)
        e_pieces = _split_bf16(s.e_row, 2) + [jnp.zeros((6, L), BF16)]
        ksum = jnp.dot(jnp.concatenate(e_pieces, axis=0), s.k, preferred_element_type=F32)
        c_ref[s.state] = s.decay * s.c_prev + jnp.dot(wk_t, s.v, preferred_element_type=F32)
        n_ref[s.state] = s.decay * s.n_prev + ksum[0:1] + ksum[1:2]
        m_ref[s.state] = s.m_new


def _mlstm(qk, z, gates_row, bias_row, *, batch, seq):
    L = ML_CHUNK * ML_CHUNKS_PER_STEP
    assert seq % L == 0
    nc = seq // L
    n_state = 2 * ML_HEADS
    fwd = lambda b, c: b * nc + c
    bwd = lambda b, c: b * nc + nc - 1 - c

    def chunk_specs(direction, tok):
        return [
            pl.BlockSpec((L, ML_WIDTH), lambda b, c: (tok(b, c), 0)),
            pl.BlockSpec((L, ML_WIDTH), lambda b, c: (tok(b, c), 1)),
            pl.BlockSpec((L, UNIT), lambda b, c: (tok(b, c), U_V_ML)),
            pl.BlockSpec((None, 2 * ML_HEADS, L), lambda b, c: (direction, 0, tok(b, c))),
        ]

    out_sds = jax.ShapeDtypeStruct((batch * seq, ML_WIDTH), BF16)
    return pl.pallas_call(
        _mlstm_kernel,
        grid=(batch, nc),
        in_specs=chunk_specs(0, fwd) + chunk_specs(1, bwd) + [
            pl.BlockSpec((2, 2 * ML_HEADS, 1), lambda b, c: (0, 0, 0)),
        ],
        out_specs=[
            pl.BlockSpec((L, ML_WIDTH), lambda b, c: (fwd(b, c), 0)),
            pl.BlockSpec((L, ML_WIDTH), lambda b, c: (bwd(b, c), 0)),
        ],
        out_shape=[out_sds, out_sds],
        scratch_shapes=[
            pltpu.VMEM((n_state, ML_HEAD_DIM, ML_HEAD_DIM), F32),
            pltpu.VMEM((n_state, 1, ML_HEAD_DIM), F32),
            pltpu.VMEM((n_state, 1, 1), F32),
        ],
        compiler_params=_cparams(2),
        name="mlstm",
    )(qk, qk, z, gates_row, qk, qk, z, gates_row, bias_row)


def _merge_ln_kernel(x_ref, hf_ref, hb_ref, o_ref, ya_ref, ga_ref, gm_ref, ng_ref,
                     wpa_ref, wpm_ref, wout_ref, g_ref, b_ref, out_ref):
    h = hf_ref[...].astype(F32) + hb_ref[...].astype(F32)
    parts = []
    for hh in range(ML_HEADS):
        sl = slice(hh * ML_HEAD_DIM, (hh + 1) * ML_HEAD_DIM)
        hs = h[:, sl]
        mu = jnp.mean(hs, axis=-1, keepdims=True)
        hc = hs - mu
        var = jnp.mean(hc * hc, axis=-1, keepdims=True)
        parts.append(hc * lax.rsqrt(var + LN_EPS))
    y_m = jnp.concatenate(parts, axis=-1) * ng_ref[...] * _sigmoid(o_ref[...].astype(F32))
    pa = jnp.dot(ya_ref[...], wpa_ref[...], preferred_element_type=F32)
    pm = jnp.dot(y_m.astype(BF16), wpm_ref[...], preferred_element_type=F32)
    merged = _sigmoid(ga_ref[...].astype(F32)) * pa + _sigmoid(gm_ref[...].astype(F32)) * pm
    mix = jnp.dot(merged.astype(BF16), wout_ref[...], preferred_element_type=F32)
    out_ref[...] = _layernorm_rows(ALPHA * x_ref[...] + mix, g_ref[...], b_ref[...])


def _merge_ln(x, h_fwd, h_bwd, z, y_a, norm_g, w_pa, w_pm, w_out, g, b, *, tm=512):
    t, d = x.shape
    assert t % tm == 0 and U_G_NA % 2 == 0 and U_G_ML % 2 == 0
    wide = 2 * UNIT
    const = lambda i: (0, 0)
    return pl.pallas_call(
        _merge_ln_kernel,
        grid=(t // tm,),
        in_specs=[
            pl.BlockSpec((tm, d), lambda i: (i, 0)),
            pl.BlockSpec((tm, ML_WIDTH), lambda i: (i, 0)),
            pl.BlockSpec((tm, ML_WIDTH), lambda i: (i, 0)),
            pl.BlockSpec((tm, UNIT), lambda i: (i, U_O_ML)),
            pl.BlockSpec((tm, NA_WIDTH), lambda i: (i, 0)),
            pl.BlockSpec((tm, wide), lambda i: (i, U_G_NA // 2)),
            pl.BlockSpec((tm, wide), lambda i: (i, U_G_ML // 2)),
            pl.BlockSpec((1, ML_WIDTH), const),
            pl.BlockSpec((NA_WIDTH, d), const, pipeline_mode=pl.Buffered(1)),
            pl.BlockSpec((ML_WIDTH, d), const, pipeline_mode=pl.Buffered(1)),
            pl.BlockSpec((d, d), const, pipeline_mode=pl.Buffered(1)),
            pl.BlockSpec((1, d), const),
            pl.BlockSpec((1, d), const),
        ],
        out_specs=pl.BlockSpec((tm, d), lambda i: (i, 0)),
        out_shape=jax.ShapeDtypeStruct((t, d), F32),
        compiler_params=_cparams(1),
        name="merge_ln",
    )(x, h_fwd, h_bwd, z, y_a, z, z, norm_g, w_pa, w_pm, w_out, g, b)


def _prepare_params(ffa_w_gu, ffa_w_down, norm_a_g, norm_a_b, mix_w_in, na_rpb, ml_conv_w, ml_conv_b,
                    ml_gate_b, ml_norm_g, mix_w_pa, mix_w_pm, mix_w_out, norm_m_g, norm_m_b,
                    ffb_w_gu, ffb_w_down, norm_b_g, norm_b_b, l):
    sizes = (NA_WIDTH, NA_WIDTH, NA_WIDTH, 2 * ML_WIDTH, ML_WIDTH, ML_WIDTH, N_GATE, D_MODEL, D_MODEL)
    points = np.cumsum(sizes)[:-1].tolist()
    w_qa, w_ka, w_va, w_qkm, w_vm, w_om, w_gates, w_ga, w_gm = jnp.split(
        mix_w_in[l].astype(BF16), points, axis=1)
    w_in = jnp.concatenate([w_ga, w_gm, w_qa, w_ka, w_va, w_qkm, w_vm, w_om], axis=1)
    assert w_in.shape[1] == N_UNITS * UNIT
    w_gate = jnp.pad(w_gates, ((0, 0), (0, GATE_PAD - N_GATE)))
    gb = ml_gate_b[l].astype(F32).reshape(2, 2, ML_HEADS).transpose(1, 0, 2).reshape(2, 2 * ML_HEADS)
    row = lambda v: v.reshape(1, -1).astype(F32)
    return dict(
        ffa_w_gu=ffa_w_gu[l].astype(BF16), ffa_w_down=ffa_w_down[l].astype(BF16),
        norm_a_g=row(norm_a_g[l]), norm_a_b=row(norm_a_b[l]),
        w_in=w_in, w_gate=w_gate,
        na_tables=_na_bias_tables(na_rpb[l]),
        conv_w=ml_conv_w[l].astype(F32), conv_b=row(ml_conv_b[l]),
        gate_b_row=gb.reshape(2, 2 * ML_HEADS, 1),
        ml_norm_g=row(ml_norm_g[l]),
        w_pa=mix_w_pa[l].astype(BF16), w_pm=mix_w_pm[l].astype(BF16), w_out=mix_w_out[l].astype(BF16),
        norm_m_g=row(norm_m_g[l]), norm_m_b=row(norm_m_b[l]),
        ffb_w_gu=ffb_w_gu[l].astype(BF16), ffb_w_down=ffb_w_down[l].astype(BF16),
        norm_b_g=row(norm_b_g[l]), norm_b_b=row(norm_b_b[l]),
    )


def _encoder_layer(x, p):
    batch, seq, d = x.shape
    t = batch * seq
    x0 = x.reshape(t, d)
    x1 = _ffn_ln(x0, p["ffa_w_gu"], p["ffa_w_down"], p["norm_a_g"], p["norm_a_b"])
    z, gates = _in_proj(x1, p["w_in"], p["w_gate"])
    y_a = _neighbourhood_attention(z, p["na_tables"], batch=batch, seq=seq)
    qk = _conv_silu(z, p["conv_w"], p["conv_b"], batch=batch, seq=seq)
    gd = gates[:, :N_GATE].reshape(t, 2, 2, ML_HEADS)
    gates_row = gd.transpose(2, 1, 3, 0).reshape(2, 2 * ML_HEADS, t)
    h_fwd, h_bwd = _mlstm(qk, z, gates_row, p["gate_b_row"], batch=batch, seq=seq)
    x2 = _merge_ln(x1, h_fwd, h_bwd, z, y_a, p["ml_norm_g"], p["w_pa"], p["w_pm"], p["w_out"],
                   p["norm_m_g"], p["norm_m_b"])
    y = _ffn_ln(x2, p["ffb_w_gu"], p["ffb_w_down"], p["norm_b_g"], p["norm_b_b"])
    return y.reshape(batch, seq, d)


def kernel(x_prompt, x_sample, ffa_w_gu, ffa_w_down, norm_a_g, norm_a_b, mix_w_in, na_rpb, ml_conv_w, ml_conv_b, ml_gate_b, ml_norm_g, mix_w_pa, mix_w_pm, mix_w_out, norm_m_g, norm_m_b, ffb_w_gu, ffb_w_down, norm_b_g, norm_b_b):
    assert ffa_w_gu.shape[0] == DEPTH
    y_prompt, y_sample = x_prompt, x_sample
    for l in range(DEPTH):
        p = _prepare_params(ffa_w_gu, ffa_w_down, norm_a_g, norm_a_b, mix_w_in, na_rpb, ml_conv_w,
                            ml_conv_b, ml_gate_b, ml_norm_g, mix_w_pa, mix_w_pm, mix_w_out, norm_m_g,
                            norm_m_b, ffb_w_gu, ffb_w_down, norm_b_g, norm_b_b, l)
        y_prompt = _encoder_layer(y_prompt, p)
        y_sample = _encoder_layer(y_sample, p)
    return (y_prompt, y_sample)
```
